```python
import jax
import jax.numpy as jnp
from jax import lax
import numpy as np

D_MODEL = 2048
BATCH = 16
SEQ = 2048
DEPTH = 4

D_MIX = D_MODEL
HEAD_DIM = 128
D_MLSTM = D_MIX // 2
D_FOX = D_MIX // 4
D_CONV = D_MIX - D_MLSTM - D_FOX
MLSTM_HEADS = D_MLSTM // HEAD_DIM
FOX_HEADS = D_FOX // HEAD_DIM
MLSTM_CONV_WIDTH = 4
CONV_WIDTH = 31
D_FF = 4 * D_MODEL
MLSTM_CHUNK = 128
FOX_Q_BLOCK = 128
EPS = 1e-6
IN_SPLITS = (D_MLSTM, D_MLSTM, D_MLSTM, D_MLSTM, MLSTM_HEADS, MLSTM_HEADS,
             D_FOX, D_FOX, D_FOX, FOX_HEADS, D_CONV, D_CONV)
D_IN = 4 * D_MLSTM + 2 * MLSTM_HEADS + 3 * D_FOX + FOX_HEADS + 2 * D_CONV
MLSTM_F_OFF = 4 * D_MLSTM + MLSTM_HEADS
FOX_F_OFF = 4 * D_MLSTM + 2 * MLSTM_HEADS + 3 * D_FOX

kernel_name = "hybrid_mlstm_fox_conformer_trunk"


def rmsnorm(x, g):
    xf = x.astype(jnp.float32)
    y = xf * lax.rsqrt(jnp.mean(xf * xf, axis=-1, keepdims=True) + EPS)
    return (y * g.astype(jnp.float32)).astype(x.dtype)


def layernorm(x, g, b):
    xf = x.astype(jnp.float32)
    mu = jnp.mean(xf, axis=-1, keepdims=True)
    xc = xf - mu
    y = xc * lax.rsqrt(jnp.mean(xc * xc, axis=-1, keepdims=True) + EPS)
    return (y * g.astype(jnp.float32) + b.astype(jnp.float32)).astype(x.dtype)


def causal_depthwise_conv(x, w, b):
    width = w.shape[0]
    y = lax.conv_general_dilated(
        x, w[:, None, :].astype(x.dtype), window_strides=(1,),
        padding=[(width - 1, 0)], dimension_numbers=('NWC', 'WIO', 'NWC'),
        feature_group_count=x.shape[-1])
    return y + b.astype(x.dtype)


def mlstm_chunkwise(q, k, v, i_pre, f_pre):
    B, S, H, Dh = q.shape
    L = MLSTM_CHUNK
    NC = S // L

    def chunks(a):
        a = a.reshape((B, NC, L, H) + a.shape[3:])
        return jnp.moveaxis(a, (1, 3), (0, 2))

    qc = chunks(q)
    kc = chunks(k * (Dh ** -0.5))
    vc = chunks(v)
    logf = chunks(jax.nn.log_sigmoid(f_pre.astype(jnp.float32)))
    ig = chunks(i_pre.astype(jnp.float32))
    bcum = jnp.cumsum(logf, axis=-1)
    mask = jnp.tril(jnp.ones((L, L), dtype=bool))

    def step(carry, inp):
        C, n, m = carry
        qb, kb, vb, bb, ib = inp
        d_log = bb[..., :, None] - bb[..., None, :] + ib[..., None, :]
        d_log = jnp.where(mask, d_log, -jnp.inf)
        inter = bb + m[..., None]
        m_t = jnp.maximum(inter, jnp.max(d_log, axis=-1))
        w_intra = jnp.exp(d_log - m_t[..., None])
        w_inter = jnp.exp(inter - m_t)
        s = jnp.einsum('bhld,bhsd->bhls', qb, kb) * w_intra
        num = (w_inter[..., None] * jnp.einsum('bhld,bhde->bhle', qb, C)
               + jnp.einsum('bhls,bhse->bhle', s, vb))
        den = w_inter * jnp.einsum('bhld,bhd->bhl', qb, n) + jnp.sum(s, axis=-1)
        h = num / jnp.maximum(jnp.abs(den), jnp.exp(-m_t))[..., None]
        b_last = bb[..., -1]
        g = b_last[..., None] - bb + ib
        m_new = jnp.maximum(b_last + m, jnp.max(g, axis=-1))
        w_k = jnp.exp(g - m_new[..., None])
        decay = jnp.exp(b_last + m - m_new)
        C_new = decay[..., None, None] * C + jnp.einsum('bhs,bhsd,bhse->bhde', w_k, kb, vb)
        n_new = decay[..., None] * n + jnp.einsum('bhs,bhsd->bhd', w_k, kb)
        return (C_new, n_new, m_new), h

    init = (jnp.zeros((B, H, Dh, Dh), jnp.float32),
            jnp.zeros((B, H, Dh), jnp.float32),
            jnp.zeros((B, H), jnp.float32))
    _, h = lax.scan(step, init, (qc, kc, vc, bcum, ig))
    h = jnp.moveaxis(h, (0, 2), (1, 3)).reshape(B, S, H, Dh)
    return h


def forgetting_attention(q, k, v, f_pre):
    B, S, H, Dh = q.shape
    q, k, v = (jnp.swapaxes(a, 1, 2) for a in (q, k, v))
    F = jnp.swapaxes(jnp.cumsum(jax.nn.log_sigmoid(f_pre.astype(jnp.float32)), axis=1), 1, 2)
    scale = Dh ** -0.5
    outs = []
    for blk in range(S // FOX_Q_BLOCK):
        lo = blk * FOX_Q_BLOCK
        hi = lo + FOX_Q_BLOCK
        qb = q[:, :, lo:hi]
        kb = k[:, :, :hi]
        vb = v[:, :, :hi]
        logits = (jnp.einsum('bhqd,bhkd->bhqk', qb, kb).astype(jnp.float32) * scale
                  + F[:, :, lo:hi, None] - F[:, :, None, :hi])
        causal = (lo + jnp.arange(FOX_Q_BLOCK))[:, None] >= jnp.arange(hi)[None, :]
        p = jax.nn.softmax(jnp.where(causal, logits, -jnp.inf), axis=-1)
        outs.append(jnp.einsum('bhqk,bhkd->bhqd', p.astype(vb.dtype), vb))
    o = jnp.concatenate(outs, axis=2)
    return jnp.swapaxes(o, 1, 2).reshape(B, S, H * Dh)


def conformer_conv(u, g, w_dw, b_dw, ln_g, ln_b):
    y = u * jax.nn.sigmoid(g)
    y = causal_depthwise_conv(y, w_dw, b_dw)
    y = layernorm(y, ln_g, ln_b)
    return jax.nn.silu(y)


def hybrid_layer(x, norm_mix, w_in, b_in, mlstm_conv_w, mlstm_conv_b, mlstm_head_norm,
                 fox_head_norm, conv_dw_w, conv_dw_b, conv_ln_g, conv_ln_b, w_out,
                 norm_ffn, w_up, w_down):
    B, S, _ = x.shape
    h = rmsnorm(x, norm_mix)
    proj = h @ w_in + b_in
    split_points = np.cumsum(IN_SPLITS)[:-1].tolist()
    (q_m, k_m, v_m, o_m, i_m, f_m, q_a, k_a, v_a, f_a, u_c, g_c) = jnp.split(proj, split_points, axis=-1)

    qk = jax.nn.silu(causal_depthwise_conv(jnp.concatenate([q_m, k_m], axis=-1), mlstm_conv_w, mlstm_conv_b))
    q_m, k_m = jnp.split(qk, 2, axis=-1)
    heads_m = lambda a: a.reshape(B, S, MLSTM_HEADS, HEAD_DIM)
    hm = mlstm_chunkwise(heads_m(q_m), heads_m(k_m), heads_m(v_m), i_m, f_m).astype(x.dtype)
    y_m = rmsnorm(hm, mlstm_head_norm).reshape(B, S, D_MLSTM) * jax.nn.sigmoid(o_m)

    heads_a = lambda a: a.reshape(B, S, FOX_HEADS, HEAD_DIM)
    ha = forgetting_attention(heads_a(q_a), heads_a(k_a), heads_a(v_a), f_a)
    y_a = rmsnorm(heads_a(ha), fox_head_norm).reshape(B, S, D_FOX)

    y_c = conformer_conv(u_c, g_c, conv_dw_w, conv_dw_b, conv_ln_g, conv_ln_b)

    x = x + jnp.concatenate([y_m, y_a, y_c], axis=-1) @ w_out
    h2 = rmsnorm(x, norm_ffn)
    x = x + jnp.square(jax.nn.relu(h2 @ w_up)) @ w_down
    return x


def setup_inputs(seed: int = 0) -> dict:
    key = jax.random.key(seed)
    ks = jax.random.split(key, 17)
    f32 = jnp.float32

    def nrm(k, shape, scale):
        return jax.random.normal(k, shape, f32) * scale

    def gain(k, shape):
        return 1.0 + nrm(k, shape, 0.02)

    x = nrm(ks[0], (BATCH, SEQ, D_MODEL), 1.0)
    norm_mix = gain(ks[1], (DEPTH, D_MODEL))
    w_in = nrm(ks[2], (DEPTH, D_MODEL, D_IN), D_MODEL ** -0.5)
    b_in = nrm(ks[3], (DEPTH, D_IN), 0.02)
    b_in = b_in.at[:, MLSTM_F_OFF:MLSTM_F_OFF + MLSTM_HEADS].add(jnp.linspace(3.0, 6.0, MLSTM_HEADS))
    b_in = b_in.at[:, FOX_F_OFF:FOX_F_OFF + FOX_HEADS].add(jnp.linspace(1.0, 4.0, FOX_HEADS))
    mlstm_conv_w = nrm(ks[4], (DEPTH, MLSTM_CONV_WIDTH, 2 * D_MLSTM), MLSTM_CONV_WIDTH ** -0.5)
    mlstm_conv_b = nrm(ks[5], (DEPTH, 2 * D_MLSTM), 0.02)
    mlstm_head_norm = gain(ks[6], (DEPTH, MLSTM_HEADS, HEAD_DIM))
    fox_head_norm = gain(ks[7], (DEPTH, FOX_HEADS, HEAD_DIM))
    conv_dw_w = nrm(ks[8], (DEPTH, CONV_WIDTH, D_CONV), CONV_WIDTH ** -0.5)
    conv_dw_b = nrm(ks[9], (DEPTH, D_CONV), 0.02)
    conv_ln_g = gain(ks[10], (DEPTH, D_CONV))
    conv_ln_b = nrm(ks[11], (DEPTH, D_CONV), 0.02)
    w_out = nrm(ks[12], (DEPTH, D_MIX, D_MODEL), D_MIX ** -0.5)
    norm_ffn = gain(ks[13], (DEPTH, D_MODEL))
    w_up = nrm(ks[14], (DEPTH, D_MODEL, D_FF), D_MODEL ** -0.5)
    w_down = nrm(ks[15], (DEPTH, D_FF, D_MODEL), D_FF ** -0.5)
    final_norm = gain(ks[16], (D_MODEL,))
    return {"x": x, "norm_mix": norm_mix, "w_in": w_in, "b_in": b_in,
            "mlstm_conv_w": mlstm_conv_w, "mlstm_conv_b": mlstm_conv_b,
            "mlstm_head_norm": mlstm_head_norm, "fox_head_norm": fox_head_norm,
            "conv_dw_w": conv_dw_w, "conv_dw_b": conv_dw_b,
            "conv_ln_g": conv_ln_g, "conv_ln_b": conv_ln_b, "w_out": w_out,
            "norm_ffn": norm_ffn, "w_up": w_up, "w_down": w_down,
            "final_norm": final_norm}


def reference(x, norm_mix, w_in, b_in, mlstm_conv_w, mlstm_conv_b, mlstm_head_norm,
              fox_head_norm, conv_dw_w, conv_dw_b, conv_ln_g, conv_ln_b, w_out,
              norm_ffn, w_up, w_down, final_norm):
    for l in range(DEPTH):
        x = hybrid_layer(x, norm_mix[l], w_in[l], b_in[l], mlstm_conv_w[l], mlstm_conv_b[l],
                         mlstm_head_norm[l], fox_head_norm[l], conv_dw_w[l], conv_dw_b[l],
                         conv_ln_g[l], conv_ln_b[l], w_out[l], norm_ffn[l], w_up[l], w_down[l])
    return rmsnorm(x, final_norm)
```

```python
import functools

import jax
import jax.numpy as jnp
import numpy as np
from jax import lax
from jax.experimental import pallas as pl
from jax.experimental.pallas import tpu as pltpu

D_MODEL = 2048
DEPTH = 4
HEAD_DIM = 128
D_MLSTM = D_MODEL // 2
D_FOX = D_MODEL // 4
D_CONV = D_MODEL - D_MLSTM - D_FOX
MLSTM_HEADS = D_MLSTM // HEAD_DIM
FOX_HEADS = D_FOX // HEAD_DIM
MLSTM_CONV_WIDTH = 4
CONV_WIDTH = 31
D_FF = 4 * D_MODEL
MLSTM_CHUNK = 128
EPS = 1e-6

LANES = 128
SUBLANES = 8
VMEM_LIMIT = 56 * 1024 * 1024

D_MAIN = 4 * D_MLSTM + 3 * D_FOX + 2 * D_CONV
GATE_I = 0
GATE_F = MLSTM_HEADS
GATE_FA = 2 * MLSTM_HEADS
BLK_QA, BLK_KA, BLK_VA, BLK_UC, BLK_GC = 8, 9, 10, 11, 12

BF16 = jnp.bfloat16
F32 = jnp.float32
HIGHEST = lax.Precision.HIGHEST
NT_DIMS = (((1,), (1,)), ((), ()))
TN_DIMS = (((0,), (0,)), ((), ()))


def _rms_rows(x, g):
    ms = jnp.mean(x * x, axis=-1, keepdims=True)
    return x * lax.rsqrt(ms + EPS) * g


def _sigmoid(x):
    return 1.0 / (1.0 + jnp.exp(-x))


def _log_sigmoid(x):
    return jnp.minimum(x, 0.0) - jnp.log1p(jnp.exp(-jnp.abs(x)))


def _tril_f32(n):
    r = lax.broadcasted_iota(jnp.int32, (n, n), 0)
    c = lax.broadcasted_iota(jnp.int32, (n, n), 1)
    return (r >= c).astype(F32)


IN_TM = 1024
IN_TN = 512
IN_SLAB = 128


def _in_proj_kernel(x_ref, g_ref, w_ref, b_ref, wg_ref, bg_ref, out_ref, gates_ref, xn_ref):
    @pl.when(pl.program_id(1) == 0)
    def _():
        def slab(r, carry):
            rows = pl.ds(pl.multiple_of(r * IN_SLAB, IN_SLAB), IN_SLAB)
            xn = _rms_rows(x_ref[rows, :], g_ref[...]).astype(BF16)
            xn_ref[rows, :] = xn
            gates_ref[rows, :] = (
                jnp.dot(xn, wg_ref[...], preferred_element_type=F32) + bg_ref[...])
            return carry
        lax.fori_loop(0, IN_TM // IN_SLAB, slab, 0)

    acc = jnp.dot(xn_ref[...], w_ref[...], preferred_element_type=F32)
    out_ref[...] = (acc + b_ref[...]).astype(out_ref.dtype)


def _in_proj(x2, g, w_main, b_main, w_gate, b_gate):
    m = x2.shape[0]
    return pl.pallas_call(
        _in_proj_kernel,
        grid=(m // IN_TM, D_MAIN // IN_TN),
        in_specs=[
            pl.BlockSpec((IN_TM, D_MODEL), lambda i, j: (i, 0)),
            pl.BlockSpec((1, D_MODEL), lambda i, j: (0, 0)),
            pl.BlockSpec((D_MODEL, IN_TN), lambda i, j: (0, j)),
            pl.BlockSpec((1, IN_TN), lambda i, j: (0, j)),
            pl.BlockSpec((D_MODEL, LANES), lambda i, j: (0, 0)),
            pl.BlockSpec((1, LANES), lambda i, j: (0, 0)),
        ],
        out_specs=[
            pl.BlockSpec((IN_TM, IN_TN), lambda i, j: (i, j)),
            pl.BlockSpec((IN_TM, LANES), lambda i, j: (i, 0)),
        ],
        out_shape=[
            jax.ShapeDtypeStruct((m, D_MAIN), BF16),
            jax.ShapeDtypeStruct((m, LANES), F32),
        ],
        scratch_shapes=[pltpu.VMEM((IN_TM, D_MODEL), BF16)],
        compiler_params=pltpu.CompilerParams(
            dimension_semantics=("parallel", "arbitrary"),
            vmem_limit_bytes=VMEM_LIMIT),
        name="in_proj",
    )(x2, g, w_main, b_main, w_gate, b_gate)


CONV_PAD = SUBLANES


def _mlstm_kernel(q_ref, k_ref, v_ref, o_ref, gt_ref, wq_ref, wk_ref, bq_ref, bk_ref,
                  hn_ref, y_ref, qbuf, kbuf, c_ref, m_ref):
    L = MLSTM_CHUNK
    c = pl.program_id(1)

    @pl.when(c == 0)
    def _():
        qbuf[0:CONV_PAD, :] = jnp.zeros((CONV_PAD, D_MLSTM), F32)
        kbuf[0:CONV_PAD, :] = jnp.zeros((CONV_PAD, D_MLSTM), F32)
        c_ref[...] = jnp.zeros_like(c_ref)
        m_ref[...] = jnp.zeros_like(m_ref)

    @pl.when(c > 0)
    def _():
        qbuf[0:CONV_PAD, :] = qbuf[L:L + CONV_PAD, :]
        kbuf[0:CONV_PAD, :] = kbuf[L:L + CONV_PAD, :]

    qbuf[CONV_PAD:CONV_PAD + L, :] = q_ref[...].astype(F32)
    kbuf[CONV_PAD:CONV_PAD + L, :] = k_ref[...].astype(F32)

    gates = gt_ref[...]
    bcum = jnp.dot(_tril_f32(L), _log_sigmoid(gates), precision=HIGHEST,
                   preferred_element_type=F32)
    gates_t = gates.T
    bcum_t = bcum.T
    row = lax.broadcasted_iota(jnp.int32, (L, L), 0)
    col = lax.broadcasted_iota(jnp.int32, (L, L), 1)
    causal = row >= col
    ones_ext = jnp.ones((L, HEAD_DIM), BF16)

    def conv_swish(buf, w_ref, b_ref, hs):
        acc = b_ref[:, hs] + w_ref[MLSTM_CONV_WIDTH - 1:MLSTM_CONV_WIDTH, hs] * buf[CONV_PAD:CONV_PAD + L, hs]
        for j in range(MLSTM_CONV_WIDTH - 1):
            off = CONV_PAD - (MLSTM_CONV_WIDTH - 1) + j
            acc = acc + w_ref[j:j + 1, hs] * buf[off:off + L, hs]
        return acc * _sigmoid(acc)

    for h in range(MLSTM_HEADS):
        hs = slice(h * HEAD_DIM, (h + 1) * HEAD_DIM)
        b_c = jnp.broadcast_to(bcum[:, GATE_F + h:GATE_F + h + 1], (L, L))
        i_c = jnp.broadcast_to(gates[:, GATE_I + h:GATE_I + h + 1], (L, L))
        b_r = bcum_t[GATE_F + h:GATE_F + h + 1, :]
        i_r = gates_t[GATE_I + h:GATE_I + h + 1, :]
        m_prev = m_ref[h, 0:1, :]

        d_log = jnp.where(causal, b_c - b_r + i_r, -jnp.inf)
        inter = b_c + m_prev
        m_t = jnp.maximum(inter, jnp.max(d_log, axis=-1, keepdims=True))
        w_intra = jnp.exp(d_log - m_t)
        w_inter = jnp.exp(inter - m_t)

        qh = conv_swish(qbuf, wq_ref, bq_ref, hs)
        kh = conv_swish(kbuf, wk_ref, bk_ref, hs) * (HEAD_DIM ** -0.5)
        qb = qh.astype(BF16)
        kb = kh.astype(BF16)
        v_ext = jnp.concatenate([v_ref[:, hs], ones_ext], axis=-1)
        c_ext = c_ref[h]

        s = lax.dot_general(qb, kb, NT_DIMS, preferred_element_type=F32) * w_intra
        qc = jnp.dot(qb, c_ext.astype(BF16), preferred_element_type=F32)
        sv = jnp.dot(s.astype(BF16), v_ext, preferred_element_type=F32)
        nx = jnp.concatenate([w_inter, w_inter], axis=-1) * qc + sv
        num = nx[:, :HEAD_DIM]
        den = nx[:, HEAD_DIM:]
        hh = num / jnp.maximum(jnp.abs(den), jnp.exp(-m_t))

        b_last = b_c[L - 1:L, :]
        g_c = b_last - b_c + i_c
        m_new = jnp.maximum(b_last + m_prev, jnp.max(g_c, axis=0, keepdims=True))
        w_k = jnp.exp(g_c - m_new)
        decay = jnp.exp(b_last + m_prev - m_new)
        kw = (kh * w_k).astype(BF16)
        upd = lax.dot_general(kw, v_ext, TN_DIMS, preferred_element_type=F32)
        c_ref[h] = jnp.concatenate([decay, decay], axis=-1) * c_ext + upd
        m_ref[h] = jnp.broadcast_to(m_new, (SUBLANES, LANES))

        hn = _rms_rows(hh, hn_ref[:, hs])
        y_ref[:, hs] = (hn * _sigmoid(o_ref[:, hs].astype(F32))).astype(y_ref.dtype)


def _mlstm(main, gates, conv_w, conv_b, head_norm, batch, seq):
    L = MLSTM_CHUNK
    nc = seq // L
    m = main.shape[0]
    row = lambda b, c: b * nc + c
    return pl.pallas_call(
        _mlstm_kernel,
        grid=(batch, nc),
        in_specs=[
            pl.BlockSpec((L, D_MLSTM), lambda b, c: (row(b, c), 0)),
            pl.BlockSpec((L, D_MLSTM), lambda b, c: (row(b, c), 1)),
            pl.BlockSpec((L, D_MLSTM), lambda b, c: (row(b, c), 2)),
            pl.BlockSpec((L, D_MLSTM), lambda b, c: (row(b, c), 3)),
            pl.BlockSpec((L, LANES), lambda b, c: (row(b, c), 0)),
            pl.BlockSpec((MLSTM_CONV_WIDTH, D_MLSTM), lambda b, c: (0, 0)),
            pl.BlockSpec((MLSTM_CONV_WIDTH, D_MLSTM), lambda b, c: (0, 1)),
            pl.BlockSpec((1, D_MLSTM), lambda b, c: (0, 0)),
            pl.BlockSpec((1, D_MLSTM), lambda b, c: (0, 1)),
            pl.BlockSpec((1, D_MLSTM), lambda b, c: (0, 0)),
        ],
        out_specs=pl.BlockSpec((L, D_MLSTM), lambda b, c: (row(b, c), 0)),
        out_shape=jax.ShapeDtypeStruct((m, D_MLSTM), BF16),
        scratch_shapes=[
            pltpu.VMEM((CONV_PAD + L, D_MLSTM), F32),
            pltpu.VMEM((CONV_PAD + L, D_MLSTM), F32),
            pltpu.VMEM((MLSTM_HEADS, HEAD_DIM, 2 * HEAD_DIM), F32),
            pltpu.VMEM((MLSTM_HEADS, SUBLANES, LANES), F32),
        ],
        compiler_params=pltpu.CompilerParams(
            dimension_semantics=("parallel", "arbitrary"),
            vmem_limit_bytes=VMEM_LIMIT),
        name="mlstm",
    )(main, main, main, main, gates, conv_w, conv_w, conv_b, conv_b, head_norm)


FOX_TQ = 256
FOX_TK = 256


def _fox_kernel(q_ref, k_ref, v_ref, gt_ref, hn_ref, y_ref, fcol, frow, *, seq):
    qi = pl.program_id(1)

    @pl.when(qi == 0)
    def _():
        tril = _tril_f32(FOX_TK)
        carry = jnp.zeros((1, LANES), F32)
        for blk in range(seq // FOX_TK):
            rows = slice(blk * FOX_TK, (blk + 1) * FOX_TK)
            cs = jnp.dot(tril, _log_sigmoid(gt_ref[rows, :]), precision=HIGHEST,
                         preferred_element_type=F32) + carry
            fcol[rows, :] = cs
            carry = cs[FOX_TK - 1:FOX_TK, :]
            for half in range(FOX_TK // LANES):
                t = cs[half * LANES:(half + 1) * LANES, :].T
                lo = blk * FOX_TK + half * LANES
                frow[:, lo:lo + LANES] = t[GATE_FA:GATE_FA + SUBLANES, :]

    q_rows = pl.ds(pl.multiple_of(qi * FOX_TQ, FOX_TQ), FOX_TQ)
    row = lax.broadcasted_iota(jnp.int32, (FOX_TQ, FOX_TK), 0)
    col = lax.broadcasted_iota(jnp.int32, (FOX_TQ, FOX_TK), 1)
    scale = HEAD_DIM ** -0.5

    for h in range(FOX_HEADS):
        hs = slice(h * HEAD_DIM, (h + 1) * HEAD_DIM)
        qb = q_ref[:, hs]
        f_q = fcol[q_rows, GATE_FA + h:GATE_FA + h + 1]

        def body(kb, carry, hs=hs, h=h, qb=qb, f_q=f_q):
            m_i, l_i, acc = carry
            ks = pl.ds(pl.multiple_of(kb * FOX_TK, FOX_TK), FOX_TK)
            f_k = frow[h:h + 1, ks]
            logits = lax.dot_general(qb, k_ref[ks, hs], NT_DIMS,
                                     preferred_element_type=F32) * scale + f_q - f_k
            visible = (row + qi * FOX_TQ) >= (col + kb * FOX_TK)
            logits = jnp.where(visible, logits, -jnp.inf)
            m_new = jnp.maximum(m_i, jnp.max(logits, axis=-1, keepdims=True))
            p = jnp.exp(logits - m_new)
            alpha = jnp.exp(m_i - m_new)
            l_new = alpha * l_i + jnp.sum(p, axis=-1, keepdims=True)
            acc_new = alpha * acc + jnp.dot(p.astype(BF16), v_ref[ks, hs],
                                            preferred_element_type=F32)
            return m_new, l_new, acc_new

        init = (jnp.full((FOX_TQ, 1), -jnp.inf, F32), jnp.zeros((FOX_TQ, 1), F32),
                jnp.zeros((FOX_TQ, HEAD_DIM), F32))
        _, l_i, acc = lax.fori_loop(0, qi + 1, body, init)
        o = acc / l_i
        y_ref[:, hs] = _rms_rows(o, hn_ref[:, hs]).astype(y_ref.dtype)


def _fox(main, gates, head_norm, batch, seq):
    nq = seq // FOX_TQ
    m = main.shape[0]
    return pl.pallas_call(
        functools.partial(_fox_kernel, seq=seq),
        grid=(batch, nq),
        in_specs=[
            pl.BlockSpec((FOX_TQ, D_FOX), lambda b, i: (b * nq + i, BLK_QA)),
            pl.BlockSpec((seq, D_FOX), lambda b, i: (b, BLK_KA)),
            pl.BlockSpec((seq, D_FOX), lambda b, i: (b, BLK_VA)),
            pl.BlockSpec((seq, LANES), lambda b, i: (b, 0)),
            pl.BlockSpec((1, D_FOX), lambda b, i: (0, 0)),
        ],
        out_specs=pl.BlockSpec((FOX_TQ, D_FOX), lambda b, i: (b * nq + i, 0)),
        out_shape=jax.ShapeDtypeStruct((m, D_FOX), BF16),
        scratch_shapes=[
            pltpu.VMEM((seq, LANES), F32),
            pltpu.VMEM((SUBLANES, seq), F32),
        ],
        compiler_params=pltpu.CompilerParams(
            dimension_semantics=("parallel", "arbitrary"),
            vmem_limit_bytes=VMEM_LIMIT),
        name="fox",
    )(main, main, main, gates, head_norm)


CC_TS = 256
CC_HIST = 32
CC_ROWS = 32


def _cconv_kernel(u_ref, g_ref, w_ref, b_ref, lg_ref, lb_ref, y_ref, ybuf):
    si = pl.program_id(1)

    @pl.when(si == 0)
    def _():
        ybuf[0:CC_HIST, :] = jnp.zeros((CC_HIST, D_CONV), F32)

    @pl.when(si > 0)
    def _():
        ybuf[0:CC_HIST, :] = ybuf[CC_TS:CC_TS + CC_HIST, :]

    ybuf[CC_HIST:CC_HIST + CC_TS, :] = (
        u_ref[...].astype(F32) * _sigmoid(g_ref[...].astype(F32)))

    first = CC_HIST - (CONV_WIDTH - 1)
    for r in range(CC_TS // CC_ROWS):
        base = r * CC_ROWS + first
        acc = jnp.broadcast_to(b_ref[...], (CC_ROWS, D_CONV))
        for j in range(CONV_WIDTH):
            acc = acc + w_ref[j:j + 1, :] * ybuf[base + j:base + j + CC_ROWS, :]
        mu = jnp.mean(acc, axis=-1, keepdims=True)
        xc = acc - mu
        yn = xc * lax.rsqrt(jnp.mean(xc * xc, axis=-1, keepdims=True) + EPS)
        yn = yn * lg_ref[...] + lb_ref[...]
        y_ref[r * CC_ROWS:(r + 1) * CC_ROWS, :] = (yn * _sigmoid(yn)).astype(y_ref.dtype)


def _cconv(main, w, b, ln_g, ln_b, batch, seq):
    ns = seq // CC_TS
    m = main.shape[0]
    return pl.pallas_call(
        _cconv_kernel,
        grid=(batch, ns),
        in_specs=[
            pl.BlockSpec((CC_TS, D_CONV), lambda bi, s: (bi * ns + s, BLK_UC)),
            pl.BlockSpec((CC_TS, D_CONV), lambda bi, s: (bi * ns + s, BLK_GC)),
            pl.BlockSpec((CONV_WIDTH, D_CONV), lambda bi, s: (0, 0)),
            pl.BlockSpec((1, D_CONV), lambda bi, s: (0, 0)),
            pl.BlockSpec((1, D_CONV), lambda bi, s: (0, 0)),
            pl.BlockSpec((1, D_CONV), lambda bi, s: (0, 0)),
        ],
        out_specs=pl.BlockSpec((CC_TS, D_CONV), lambda bi, s: (bi * ns + s, 0)),
        out_shape=jax.ShapeDtypeStruct((m, D_CONV), BF16),
        scratch_shapes=[pltpu.VMEM((CC_HIST + CC_TS, D_CONV), F32)],
        compiler_params=pltpu.CompilerParams(
            dimension_semantics=("parallel", "arbitrary"),
            vmem_limit_bytes=VMEM_LIMIT),
        name="cconv",
    )(main, main, w, b, ln_g, ln_b)


OUT_TM = 1024
OUT_TN = 1024


def _out_proj_kernel(x_ref, ym_ref, ya_ref, yc_ref, wm_ref, wa_ref, wc_ref, o_ref):
    acc = jnp.dot(ym_ref[...], wm_ref[...], preferred_element_type=F32)
    acc = acc + jnp.dot(ya_ref[...], wa_ref[...], preferred_element_type=F32)
    acc = acc + jnp.dot(yc_ref[...], wc_ref[...], preferred_element_type=F32)
    o_ref[...] = x_ref[...] + acc


def _out_proj(x2, ym, ya, yc, w_out):
    m = x2.shape[0]
    fox_blk = D_MLSTM // D_FOX
    return pl.pallas_call(
        _out_proj_kernel,
        grid=(m // OUT_TM, D_MODEL // OUT_TN),
        in_specs=[
            pl.BlockSpec((OUT_TM, OUT_TN), lambda i, j: (i, j)),
            pl.BlockSpec((OUT_TM, D_MLSTM), lambda i, j: (i, 0)),
            pl.BlockSpec((OUT_TM, D_FOX), lambda i, j: (i, 0)),
            pl.BlockSpec((OUT_TM, D_CONV), lambda i, j: (i, 0)),
            pl.BlockSpec((D_MLSTM, OUT_TN), lambda i, j: (0, j)),
            pl.BlockSpec((D_FOX, OUT_TN), lambda i, j: (fox_blk, j)),
            pl.BlockSpec((D_CONV, OUT_TN), lambda i, j: (fox_blk + 1, j)),
        ],
        out_specs=pl.BlockSpec((OUT_TM, OUT_TN), lambda i, j: (i, j)),
        out_shape=jax.ShapeDtypeStruct((m, D_MODEL), F32),
        compiler_params=pltpu.CompilerParams(
            dimension_semantics=("parallel", "arbitrary"),
            vmem_limit_bytes=VMEM_LIMIT),
        name="out_proj",
    )(x2, ym, ya, yc, w_out, w_out, w_out)


FFN_TM = 512
FFN_TF = 512
FFN_SLAB = 128


def _ffn_kernel(x_ref, g_ref, wu_ref, wd_ref, fg_ref, o_ref, xn_ref, *, final_norm):
    j = pl.program_id(1)

    @pl.when(j == 0)
    def _():
        def slab(r, carry):
            rows = pl.ds(pl.multiple_of(r * FFN_SLAB, FFN_SLAB), FFN_SLAB)
            x = x_ref[rows, :]
            xn_ref[rows, :] = _rms_rows(x, g_ref[...]).astype(BF16)
            o_ref[rows, :] = x
            return carry
        lax.fori_loop(0, FFN_TM // FFN_SLAB, slab, 0)

    hid = jnp.maximum(jnp.dot(xn_ref[...], wu_ref[...], preferred_element_type=F32), 0.0)
    hid = (hid * hid).astype(BF16)
    o_ref[...] += jnp.dot(hid, wd_ref[...], preferred_element_type=F32)

    if final_norm:
        @pl.when(j == pl.num_programs(1) - 1)
        def _():
            def slab(r, carry):
                rows = pl.ds(pl.multiple_of(r * FFN_SLAB, FFN_SLAB), FFN_SLAB)
                o_ref[rows, :] = _rms_rows(o_ref[rows, :], fg_ref[...])
                return carry
            lax.fori_loop(0, FFN_TM // FFN_SLAB, slab, 0)


def _ffn(x2, g, w_up, w_down, final_g, final_norm):
    m = x2.shape[0]
    return pl.pallas_call(
        functools.partial(_ffn_kernel, final_norm=final_norm),
        grid=(m // FFN_TM, D_FF // FFN_TF),
        in_specs=[
            pl.BlockSpec((FFN_TM, D_MODEL), lambda i, j: (i, 0)),
            pl.BlockSpec((1, D_MODEL), lambda i, j: (0, 0)),
            pl.BlockSpec((D_MODEL, FFN_TF), lambda i, j: (0, j)),
            pl.BlockSpec((FFN_TF, D_MODEL), lambda i, j: (j, 0)),
            pl.BlockSpec((1, D_MODEL), lambda i, j: (0, 0)),
        ],
        out_specs=pl.BlockSpec((FFN_TM, D_MODEL), lambda i, j: (i, 0)),
        out_shape=jax.ShapeDtypeStruct((m, D_MODEL), F32),
        scratch_shapes=[pltpu.VMEM((FFN_TM, D_MODEL), BF16)],
        compiler_params=pltpu.CompilerParams(
            dimension_semantics=("parallel", "arbitrary"),
            vmem_limit_bytes=VMEM_LIMIT),
        name="ffn",
    )(x2, g, w_up, w_down, final_g)


def _split_in_proj(w_in, b_in):
    o = np.cumsum((0, D_MLSTM, D_MLSTM, D_MLSTM, D_MLSTM, MLSTM_HEADS, MLSTM_HEADS,
                   D_FOX, D_FOX, D_FOX, FOX_HEADS, D_CONV, D_CONV))
    main = lambda a: jnp.concatenate([a[..., o[0]:o[4]], a[..., o[6]:o[9]], a[..., o[10]:o[12]]], axis=-1)
    pad = LANES - 2 * MLSTM_HEADS - FOX_HEADS
    gate = lambda a: jnp.concatenate(
        [a[..., o[4]:o[6]], a[..., o[9]:o[10]], jnp.zeros(a.shape[:-1] + (pad,), a.dtype)], axis=-1)
    return main(w_in).astype(BF16), main(b_in), gate(w_in).astype(BF16), gate(b_in)


def kernel(x, norm_mix, w_in, b_in, mlstm_conv_w, mlstm_conv_b, mlstm_head_norm, fox_head_norm,
           conv_dw_w, conv_dw_b, conv_ln_g, conv_ln_b, w_out, norm_ffn, w_up, w_down, final_norm):
    batch, seq, d = x.shape
    assert d == D_MODEL and seq % FOX_TQ == 0 and (batch * seq) % IN_TM == 0
    depth = w_in.shape[0]
    x2 = x.reshape(batch * seq, d)
    w_main, b_main, w_gate, b_gate = _split_in_proj(w_in, b_in)
    w_out_b = w_out.astype(BF16)
    w_up_b = w_up.astype(BF16)
    w_down_b = w_down.astype(BF16)
    row = lambda a: a.reshape(1, -1)

    for l in range(depth):
        main, gates = _in_proj(x2, row(norm_mix[l]), w_main[l], row(b_main[l]),
                               w_gate[l], row(b_gate[l]))
        ym = _mlstm(main, gates, mlstm_conv_w[l], row(mlstm_conv_b[l]),
                    row(mlstm_head_norm[l]), batch, seq)
        ya = _fox(main, gates, row(fox_head_norm[l]), batch, seq)
        yc = _cconv(main, conv_dw_w[l], row(conv_dw_b[l]), row(conv_ln_g[l]),
                    row(conv_ln_b[l]), batch, seq)
        x2 = _out_proj(x2, ym, ya, yc, w_out_b[l])
        x2 = _ffn(x2, row(norm_ffn[l]), w_up_b[l], w_down_b[l], row(final_norm),
                  final_norm=(l == depth - 1))
    return x2.reshape(batch, seq, d)
```

```python
import functools
import math

import jax
import jax.numpy as jnp
import numpy as np
from jax import lax
from jax.experimental import pallas as pl
from jax.experimental.pallas import tpu as pltpu

D_MODEL = 2048
DEPTH = 4
HEAD_DIM = 128
D_MLSTM = D_MODEL // 2
D_FOX = D_MODEL // 4
D_CONV = D_MODEL - D_MLSTM - D_FOX
MLSTM_HEADS = D_MLSTM // HEAD_DIM
FOX_HEADS = D_FOX // HEAD_DIM
MLSTM_CONV_WIDTH = 4
CONV_WIDTH = 31
D_FF = 4 * D_MODEL
MLSTM_CHUNK = 128
EPS = 1e-6

LANES = 128
SUBLANES = 8
VMEM_LIMIT = 56 * 1024 * 1024

D_MAIN = 4 * D_MLSTM + 3 * D_FOX + 2 * D_CONV
GATE_I = 0
GATE_F = MLSTM_HEADS
GATE_FA = 2 * MLSTM_HEADS
BLK_QA, BLK_KA, BLK_VA, BLK_UC, BLK_GC = 8, 9, 10, 11, 12

BF16 = jnp.bfloat16
F32 = jnp.float32
HIGHEST = lax.Precision.HIGHEST
NT_DIMS = (((1,), (1,)), ((), ()))
TN_DIMS = (((0,), (0,)), ((), ()))


def _rms_rows(x, g):
    ms = jnp.mean(x * x, axis=-1, keepdims=True)
    return x * lax.rsqrt(ms + EPS) * g


def _sigmoid(x):
    return 0.5 * jnp.tanh(0.5 * x) + 0.5


def _log_sigmoid(x):
    return jnp.minimum(x, 0.0) - jnp.log1p(jnp.exp(-jnp.abs(x)))


def _tril_f32(n):
    r = lax.broadcasted_iota(jnp.int32, (n, n), 0)
    c = lax.broadcasted_iota(jnp.int32, (n, n), 1)
    return (r >= c).astype(F32)


IN_TM = 512
IN_TN = 512
IN_SLAB = 128


def _in_proj_kernel(x_ref, g_ref, w_ref, b_ref, wg_ref, bg_ref, out_ref, gates_ref, xn_ref):
    def slab(r, carry):
        rows = pl.ds(pl.multiple_of(r * IN_SLAB, IN_SLAB), IN_SLAB)
        xn = _rms_rows(x_ref[rows, :], g_ref[...]).astype(BF16)
        xn_ref[rows, :] = xn
        gates_ref[rows, :] = (
            jnp.dot(xn, wg_ref[...], preferred_element_type=F32) + bg_ref[...])
        return carry
    lax.fori_loop(0, IN_TM // IN_SLAB, slab, 0)

    def chunk(n, carry):
        cols = pl.ds(pl.multiple_of(n * IN_TN, IN_TN), IN_TN)
        acc = jnp.dot(xn_ref[...], w_ref[:, cols], preferred_element_type=F32)
        out_ref[:, cols] = (acc + b_ref[:, cols]).astype(out_ref.dtype)
        return carry
    lax.fori_loop(0, D_MAIN // IN_TN, chunk, 0)


def _in_proj(x2, g, w_main, b_main, w_gate, b_gate):
    m = x2.shape[0]
    resident = pl.Buffered(1)
    return pl.pallas_call(
        _in_proj_kernel,
        grid=(m // IN_TM,),
        in_specs=[
            pl.BlockSpec((IN_TM, D_MODEL), lambda i: (i, 0)),
            pl.BlockSpec((1, D_MODEL), lambda i: (0, 0), pipeline_mode=resident),
            pl.BlockSpec((D_MODEL, D_MAIN), lambda i: (0, 0), pipeline_mode=resident),
            pl.BlockSpec((1, D_MAIN), lambda i: (0, 0), pipeline_mode=resident),
            pl.BlockSpec((D_MODEL, LANES), lambda i: (0, 0), pipeline_mode=resident),
            pl.BlockSpec((1, LANES), lambda i: (0, 0), pipeline_mode=resident),
        ],
        out_specs=[
            pl.BlockSpec((IN_TM, D_MAIN), lambda i: (i, 0)),
            pl.BlockSpec((IN_TM, LANES), lambda i: (i, 0)),
        ],
        out_shape=[
            jax.ShapeDtypeStruct((m, D_MAIN), BF16),
            jax.ShapeDtypeStruct((m, LANES), F32),
        ],
        scratch_shapes=[pltpu.VMEM((IN_TM, D_MODEL), BF16)],
        compiler_params=pltpu.CompilerParams(
            dimension_semantics=("parallel",),
            vmem_limit_bytes=VMEM_LIMIT),
        name="in_proj",
    )(x2, g, w_main, b_main, w_gate, b_gate)


CONV_PAD = SUBLANES


def _mlstm_kernel(q_ref, k_ref, v_ref, o_ref, gt_ref, wq_ref, wk_ref, bq_ref, bk_ref,
                  hn_ref, y_ref, qbuf, kbuf, c_ref, m_ref):
    L = MLSTM_CHUNK
    c = pl.program_id(1)

    @pl.when(c == 0)
    def _():
        qbuf[0:CONV_PAD, :] = jnp.zeros((CONV_PAD, D_MLSTM), F32)
        kbuf[0:CONV_PAD, :] = jnp.zeros((CONV_PAD, D_MLSTM), F32)
        c_ref[...] = jnp.zeros_like(c_ref)
        m_ref[...] = jnp.zeros_like(m_ref)

    @pl.when(c > 0)
    def _():
        qbuf[0:CONV_PAD, :] = qbuf[L:L + CONV_PAD, :]
        kbuf[0:CONV_PAD, :] = kbuf[L:L + CONV_PAD, :]

    qbuf[CONV_PAD:CONV_PAD + L, :] = q_ref[...].astype(F32)
    kbuf[CONV_PAD:CONV_PAD + L, :] = k_ref[...].astype(F32)

    gates = gt_ref[...]
    bcum = jnp.dot(_tril_f32(L), _log_sigmoid(gates), precision=HIGHEST,
                   preferred_element_type=F32)
    gates_t = gates.T
    bcum_t = bcum.T
    row = lax.broadcasted_iota(jnp.int32, (L, L), 0)
    col = lax.broadcasted_iota(jnp.int32, (L, L), 1)
    causal = row >= col
    ones_ext = jnp.ones((L, HEAD_DIM), BF16)

    def conv_swish(buf, w_ref, b_ref, hs):
        acc = b_ref[:, hs] + w_ref[MLSTM_CONV_WIDTH - 1:MLSTM_CONV_WIDTH, hs] * buf[CONV_PAD:CONV_PAD + L, hs]
        for j in range(MLSTM_CONV_WIDTH - 1):
            off = CONV_PAD - (MLSTM_CONV_WIDTH - 1) + j
            acc = acc + w_ref[j:j + 1, hs] * buf[off:off + L, hs]
        return acc * _sigmoid(acc)

    for h in range(MLSTM_HEADS):
        hs = slice(h * HEAD_DIM, (h + 1) * HEAD_DIM)
        b_c = jnp.broadcast_to(bcum[:, GATE_F + h:GATE_F + h + 1], (L, L))
        i_c = jnp.broadcast_to(gates[:, GATE_I + h:GATE_I + h + 1], (L, L))
        b_r = bcum_t[GATE_F + h:GATE_F + h + 1, :]
        i_r = gates_t[GATE_I + h:GATE_I + h + 1, :]
        m_prev = m_ref[h, 0:1, :]

        d_log = jnp.where(causal, b_c - b_r + i_r, -jnp.inf)
        inter = b_c + m_prev
        m_t = jnp.maximum(inter, jnp.max(d_log, axis=-1, keepdims=True))
        w_intra = jnp.exp(d_log - m_t)
        w_inter = jnp.exp(inter - m_t)

        qh = conv_swish(qbuf, wq_ref, bq_ref, hs)
        kh = conv_swish(kbuf, wk_ref, bk_ref, hs) * (HEAD_DIM ** -0.5)
        qb = qh.astype(BF16)
        kb = kh.astype(BF16)
        v_ext = jnp.concatenate([v_ref[:, hs], ones_ext], axis=-1)
        c_ext = c_ref[h]

        s = lax.dot_general(qb, kb, NT_DIMS, preferred_element_type=F32) * w_intra
        qc = jnp.dot(qb, c_ext.astype(BF16), preferred_element_type=F32)
        sv = jnp.dot(s.astype(BF16), v_ext, preferred_element_type=F32)
        nx = jnp.concatenate([w_inter, w_inter], axis=-1) * qc + sv
        num = nx[:, :HEAD_DIM]
        den = nx[:, HEAD_DIM:]
        hh = num / jnp.maximum(jnp.abs(den), jnp.exp(-m_t))

        b_last = b_c[L - 1:L, :]
        g_c = b_last - b_c + i_c
        m_new = jnp.maximum(b_last + m_prev, jnp.max(g_c, axis=0, keepdims=True))
        w_k = jnp.exp(g_c - m_new)
        decay = jnp.exp(b_last + m_prev - m_new)
        kw = (kh * w_k).astype(BF16)
        upd = lax.dot_general(kw, v_ext, TN_DIMS, preferred_element_type=F32)
        c_ref[h] = jnp.concatenate([decay, decay], axis=-1) * c_ext + upd
        m_ref[h] = jnp.broadcast_to(m_new, (SUBLANES, LANES))

        hn = _rms_rows(hh, hn_ref[:, hs])
        y_ref[:, hs] = (hn * _sigmoid(o_ref[:, hs].astype(F32))).astype(y_ref.dtype)


def _mlstm(main, gates, conv_w, conv_b, head_norm, batch, seq):
    L = MLSTM_CHUNK
    nc = seq // L
    m = main.shape[0]
    row = lambda b, c: b * nc + c
    return pl.pallas_call(
        _mlstm_kernel,
        grid=(batch, nc),
        in_specs=[
            pl.BlockSpec((L, D_MLSTM), lambda b, c: (row(b, c), 0)),
            pl.BlockSpec((L, D_MLSTM), lambda b, c: (row(b, c), 1)),
            pl.BlockSpec((L, D_MLSTM), lambda b, c: (row(b, c), 2)),
            pl.BlockSpec((L, D_MLSTM), lambda b, c: (row(b, c), 3)),
            pl.BlockSpec((L, LANES), lambda b, c: (row(b, c), 0)),
            pl.BlockSpec((MLSTM_CONV_WIDTH, D_MLSTM), lambda b, c: (0, 0)),
            pl.BlockSpec((MLSTM_CONV_WIDTH, D_MLSTM), lambda b, c: (0, 1)),
            pl.BlockSpec((1, D_MLSTM), lambda b, c: (0, 0)),
            pl.BlockSpec((1, D_MLSTM), lambda b, c: (0, 1)),
            pl.BlockSpec((1, D_MLSTM), lambda b, c: (0, 0)),
        ],
        out_specs=pl.BlockSpec((L, D_MLSTM), lambda b, c: (row(b, c), 0)),
        out_shape=jax.ShapeDtypeStruct((m, D_MLSTM), BF16),
        scratch_shapes=[
            pltpu.VMEM((CONV_PAD + L, D_MLSTM), F32),
            pltpu.VMEM((CONV_PAD + L, D_MLSTM), F32),
            pltpu.VMEM((MLSTM_HEADS, HEAD_DIM, 2 * HEAD_DIM), F32),
            pltpu.VMEM((MLSTM_HEADS, SUBLANES, LANES), F32),
        ],
        compiler_params=pltpu.CompilerParams(
            dimension_semantics=("parallel", "arbitrary"),
            vmem_limit_bytes=VMEM_LIMIT),
        name="mlstm",
    )(main, main, main, main, gates, conv_w, conv_w, conv_b, conv_b, head_norm)


FOX_TQ = 256
FOX_TK = 256
FOX_SPLIT = 3
FOX_EXP2_SCALE = (HEAD_DIM ** -0.5) * math.log2(math.e)


def _fox_bias_selectors():
    sel = np.zeros((2, FOX_HEADS, FOX_SPLIT * LANES, LANES), np.float32)
    for h in range(FOX_HEADS):
        for p in range(FOX_SPLIT):
            sel[0, h, p * LANES + GATE_FA + h, p] = 1.0
            sel[1, h, p * LANES + GATE_FA + h, FOX_SPLIT + p] = -1.0
    return jnp.asarray(sel, BF16)


def _fox_kernel(q_ref, k_ref, v_ref, gt_ref, sel_ref, hn_ref, y_ref, qbias, kbias, acc_ref, *, seq):
    qi = pl.program_id(1)
    heads = [slice(h * HEAD_DIM, (h + 1) * HEAD_DIM) for h in range(FOX_HEADS)]

    @pl.when(qi == 0)
    def _():
        tril = _tril_f32(FOX_TK)
        lane = lax.broadcasted_iota(jnp.int32, (1, LANES), 1)
        ones_k = (lane < FOX_SPLIT).astype(F32)
        ones_q = (lane < 2 * FOX_SPLIT).astype(F32) - ones_k
        carry = jnp.zeros((1, LANES), F32)
        for blk in range(seq // FOX_TK):
            rows = slice(blk * FOX_TK, (blk + 1) * FOX_TK)
            cs = jnp.dot(tril, _log_sigmoid(gt_ref[rows, :]), precision=HIGHEST,
                         preferred_element_type=F32) + carry
            carry = cs[FOX_TK - 1:FOX_TK, :]
            t = cs * (HEAD_DIM ** 0.5)
            hi = t.astype(BF16)
            rem = t - hi.astype(F32)
            mid = rem.astype(BF16)
            lo = (rem - mid.astype(F32)).astype(BF16)
            pieces = jnp.concatenate([hi, mid, lo], axis=-1)
            for h, hs in enumerate(heads):
                qbias[rows, hs] = (jnp.dot(pieces, sel_ref[0, h], preferred_element_type=F32)
                                   + ones_q).astype(BF16)
                kbias[rows, hs] = (jnp.dot(pieces, sel_ref[1, h], preferred_element_type=F32)
                                   + ones_k).astype(BF16)

    q_rows = pl.ds(pl.multiple_of(qi * FOX_TQ, FOX_TQ), FOX_TQ)
    q_aug = [jnp.concatenate([q_ref[:, hs], qbias[q_rows, hs]], axis=-1) for hs in heads]
    ones_ext = jnp.ones((FOX_TK, HEAD_DIM), BF16)
    acc_ref[...] = jnp.zeros_like(acc_ref)

    def step(kb, m_old, diagonal):
        ks = pl.ds(pl.multiple_of(kb * FOX_TK, FOX_TK), FOX_TK)
        m_out = []
        for h, hs in enumerate(heads):
            k_aug = jnp.concatenate([k_ref[ks, hs], kbias[ks, hs]], axis=-1)
            u = lax.dot_general(q_aug[h], k_aug, NT_DIMS, preferred_element_type=F32)
            if diagonal:
                row = lax.broadcasted_iota(jnp.int32, (FOX_TQ, FOX_TK), 0)
                col = lax.broadcasted_iota(jnp.int32, (FOX_TQ, FOX_TK), 1)
                u = jnp.where(row >= col, u, -jnp.inf)
            m_new = jnp.maximum(m_old[h], jnp.max(u, axis=-1, keepdims=True))
            p = jnp.exp2((u - m_new) * FOX_EXP2_SCALE)
            alpha = jnp.exp2((m_old[h] - m_new) * FOX_EXP2_SCALE)
            v_ext = jnp.concatenate([v_ref[ks, hs], ones_ext], axis=-1)
            acc_ref[h] = alpha * acc_ref[h] + jnp.dot(p.astype(BF16), v_ext,
                                                      preferred_element_type=F32)
            m_out.append(m_new)
        return tuple(m_out)

    m_init = tuple(jnp.full((FOX_TQ, 1), -jnp.inf, F32) for _ in heads)
    m_run = lax.fori_loop(0, qi, lambda kb, m: step(kb, m, False), m_init)
    step(qi, m_run, True)

    for h, hs in enumerate(heads):
        acc = acc_ref[h]
        o = acc[:, :HEAD_DIM] / acc[:, HEAD_DIM:]
        y_ref[:, hs] = _rms_rows(o, hn_ref[:, hs]).astype(y_ref.dtype)


def _fox(main, gates, head_norm, batch, seq):
    nq = seq // FOX_TQ
    m = main.shape[0]
    sel = _fox_bias_selectors()
    return pl.pallas_call(
        functools.partial(_fox_kernel, seq=seq),
        grid=(batch, nq),
        in_specs=[
            pl.BlockSpec((FOX_TQ, D_FOX), lambda b, i: (b * nq + i, BLK_QA)),
            pl.BlockSpec((seq, D_FOX), lambda b, i: (b, BLK_KA)),
            pl.BlockSpec((seq, D_FOX), lambda b, i: (b, BLK_VA)),
            pl.BlockSpec((seq, LANES), lambda b, i: (b, 0)),
            pl.BlockSpec(sel.shape, lambda b, i: (0, 0, 0, 0)),
            pl.BlockSpec((1, D_FOX), lambda b, i: (0, 0)),
        ],
        out_specs=pl.BlockSpec((FOX_TQ, D_FOX), lambda b, i: (b * nq + i, 0)),
        out_shape=jax.ShapeDtypeStruct((m, D_FOX), BF16),
        scratch_shapes=[
            pltpu.VMEM((seq, D_FOX), BF16),
            pltpu.VMEM((seq, D_FOX), BF16),
            pltpu.VMEM((FOX_HEADS, FOX_TQ, 2 * HEAD_DIM), F32),
        ],
        compiler_params=pltpu.CompilerParams(
            dimension_semantics=("parallel", "arbitrary"),
            vmem_limit_bytes=VMEM_LIMIT),
        name="fox",
    )(main, main, main, gates, sel, head_norm)


CC_TS = 256
CC_HIST = 32
CC_ROWS = 32


def _cconv_kernel(u_ref, g_ref, w_ref, b_ref, lg_ref, lb_ref, y_ref, ybuf):
    si = pl.program_id(1)

    @pl.when(si == 0)
    def _():
        ybuf[0:CC_HIST, :] = jnp.zeros((CC_HIST, D_CONV), F32)

    @pl.when(si > 0)
    def _():
        ybuf[0:CC_HIST, :] = ybuf[CC_TS:CC_TS + CC_HIST, :]

    ybuf[CC_HIST:CC_HIST + CC_TS, :] = (
        u_ref[...].astype(F32) * _sigmoid(g_ref[...].astype(F32)))

    first = CC_HIST - (CONV_WIDTH - 1)
    for r in range(CC_TS // CC_ROWS):
        base = r * CC_ROWS + first
        acc = jnp.broadcast_to(b_ref[...], (CC_ROWS, D_CONV))
        for j in range(CONV_WIDTH):
            acc = acc + w_ref[j:j + 1, :] * ybuf[base + j:base + j + CC_ROWS, :]
        mu = jnp.mean(acc, axis=-1, keepdims=True)
        xc = acc - mu
        yn = xc * lax.rsqrt(jnp.mean(xc * xc, axis=-1, keepdims=True) + EPS)
        yn = yn * lg_ref[...] + lb_ref[...]
        y_ref[r * CC_ROWS:(r + 1) * CC_ROWS, :] = (yn * _sigmoid(yn)).astype(y_ref.dtype)


def _cconv(main, w, b, ln_g, ln_b, batch, seq):
    ns = seq // CC_TS
    m = main.shape[0]
    return pl.pallas_call(
        _cconv_kernel,
        grid=(batch, ns),
        in_specs=[
            pl.BlockSpec((CC_TS, D_CONV), lambda bi, s: (bi * ns + s, BLK_UC)),
            pl.BlockSpec((CC_TS, D_CONV), lambda bi, s: (bi * ns + s, BLK_GC)),
            pl.BlockSpec((CONV_WIDTH, D_CONV), lambda bi, s: (0, 0)),
            pl.BlockSpec((1, D_CONV), lambda bi, s: (0, 0)),
            pl.BlockSpec((1, D_CONV), lambda bi, s: (0, 0)),
            pl.BlockSpec((1, D_CONV), lambda bi, s: (0, 0)),
        ],
        out_specs=pl.BlockSpec((CC_TS, D_CONV), lambda bi, s: (bi * ns + s, 0)),
        out_shape=jax.ShapeDtypeStruct((m, D_CONV), BF16),
        scratch_shapes=[pltpu.VMEM((CC_HIST + CC_TS, D_CONV), F32)],
        compiler_params=pltpu.CompilerParams(
            dimension_semantics=("parallel", "arbitrary"),
            vmem_limit_bytes=VMEM_LIMIT),
        name="cconv",
    )(main, main, w, b, ln_g, ln_b)


OUT_TM = 1024
OUT_TN = 1024


def _out_proj_kernel(x_ref, ym_ref, ya_ref, yc_ref, wm_ref, wa_ref, wc_ref, o_ref):
    acc = jnp.dot(ym_ref[...], wm_ref[...], preferred_element_type=F32)
    acc = acc + jnp.dot(ya_ref[...], wa_ref[...], preferred_element_type=F32)
    acc = acc + jnp.dot(yc_ref[...], wc_ref[...], preferred_element_type=F32)
    o_ref[...] = x_ref[...] + acc


def _out_proj(x2, ym, ya, yc, w_out):
    m = x2.shape[0]
    fox_blk = D_MLSTM // D_FOX
    return pl.pallas_call(
        _out_proj_kernel,
        grid=(m // OUT_TM, D_MODEL // OUT_TN),
        in_specs=[
            pl.BlockSpec((OUT_TM, OUT_TN), lambda i, j: (i, j)),
            pl.BlockSpec((OUT_TM, D_MLSTM), lambda i, j: (i, 0)),
            pl.BlockSpec((OUT_TM, D_FOX), lambda i, j: (i, 0)),
            pl.BlockSpec((OUT_TM, D_CONV), lambda i, j: (i, 0)),
            pl.BlockSpec((D_MLSTM, OUT_TN), lambda i, j: (0, j)),
            pl.BlockSpec((D_FOX, OUT_TN), lambda i, j: (fox_blk, j)),
            pl.BlockSpec((D_CONV, OUT_TN), lambda i, j: (fox_blk + 1, j)),
        ],
        out_specs=pl.BlockSpec((OUT_TM, OUT_TN), lambda i, j: (i, j)),
        out_shape=jax.ShapeDtypeStruct((m, D_MODEL), F32),
        compiler_params=pltpu.CompilerParams(
            dimension_semantics=("parallel", "arbitrary"),
            vmem_limit_bytes=VMEM_LIMIT),
        name="out_proj",
    )(x2, ym, ya, yc, w_out, w_out, w_out)


FFN_TM = 1024
FFN_TF = 512
FFN_SLAB = 128


def _ffn_kernel(x_ref, g_ref, wu_ref, wd_ref, fg_ref, o_ref, xn_ref, *, final_norm):
    j = pl.program_id(1)

    @pl.when(j == 0)
    def _():
        def slab(r, carry):
            rows = pl.ds(pl.multiple_of(r * FFN_SLAB, FFN_SLAB), FFN_SLAB)
            x = x_ref[rows, :]
            xn_ref[rows, :] = _rms_rows(x, g_ref[...]).astype(BF16)
            o_ref[rows, :] = x
            return carry
        lax.fori_loop(0, FFN_TM // FFN_SLAB, slab, 0)

    hid = jnp.maximum(jnp.dot(xn_ref[...], wu_ref[...], preferred_element_type=F32), 0.0)
    hid = (hid * hid).astype(BF16)
    o_ref[...] += jnp.dot(hid, wd_ref[...], preferred_element_type=F32)

    if final_norm:
        @pl.when(j == pl.num_programs(1) - 1)
        def _():
            def slab(r, carry):
                rows = pl.ds(pl.multiple_of(r * FFN_SLAB, FFN_SLAB), FFN_SLAB)
                o_ref[rows, :] = _rms_rows(o_ref[rows, :], fg_ref[...])
                return carry
            lax.fori_loop(0, FFN_TM // FFN_SLAB, slab, 0)


def _ffn(x2, g, w_up, w_down, final_g, final_norm):
    m = x2.shape[0]
    return pl.pallas_call(
        functools.partial(_ffn_kernel, final_norm=final_norm),
        grid=(m // FFN_TM, D_FF // FFN_TF),
        in_specs=[
            pl.BlockSpec((FFN_TM, D_MODEL), lambda i, j: (i, 0)),
            pl.BlockSpec((1, D_MODEL), lambda i, j: (0, 0)),
            pl.BlockSpec((D_MODEL, FFN_TF), lambda i, j: (0, j)),
            pl.BlockSpec((FFN_TF, D_MODEL), lambda i, j: (j, 0)),
            pl.BlockSpec((1, D_MODEL), lambda i, j: (0, 0)),
        ],
        out_specs=pl.BlockSpec((FFN_TM, D_MODEL), lambda i, j: (i, 0)),
        out_shape=jax.ShapeDtypeStruct((m, D_MODEL), F32),
        scratch_shapes=[pltpu.VMEM((FFN_TM, D_MODEL), BF16)],
        compiler_params=pltpu.CompilerParams(
            dimension_semantics=("parallel", "arbitrary"),
            vmem_limit_bytes=VMEM_LIMIT),
        name="ffn",
    )(x2, g, w_up, w_down, final_g)


def _split_in_proj(w_in, b_in):
    o = np.cumsum((0, D_MLSTM, D_MLSTM, D_MLSTM, D_MLSTM, MLSTM_HEADS, MLSTM_HEADS,
                   D_FOX, D_FOX, D_FOX, FOX_HEADS, D_CONV, D_CONV))
    main = lambda a: jnp.concatenate([a[..., o[0]:o[4]], a[..., o[6]:o[9]], a[..., o[10]:o[12]]], axis=-1)
    pad = LANES - 2 * MLSTM_HEADS - FOX_HEADS
    gate = lambda a: jnp.concatenate(
        [a[..., o[4]:o[6]], a[..., o[9]:o[10]], jnp.zeros(a.shape[:-1] + (pad,), a.dtype)], axis=-1)
    return main(w_in).astype(BF16), main(b_in), gate(w_in).astype(BF16), gate(b_in)


def kernel(x, norm_mix, w_in, b_in, mlstm_conv_w, mlstm_conv_b, mlstm_head_norm, fox_head_norm,
           conv_dw_w, conv_dw_b, conv_ln_g, conv_ln_b, w_out, norm_ffn, w_up, w_down, final_norm):
    batch, seq, d = x.shape
    assert d == D_MODEL and seq % FOX_TQ == 0 and (batch * seq) % IN_TM == 0
    depth = w_in.shape[0]
    x2 = x.reshape(batch * seq, d)
    w_main, b_main, w_gate, b_gate = _split_in_proj(w_in, b_in)
    w_out_b = w_out.astype(BF16)
    w_up_b = w_up.astype(BF16)
    w_down_b = w_down.astype(BF16)
    row = lambda a: a.reshape(1, -1)

    for l in range(depth):
        main, gates = _in_proj(x2, row(norm_mix[l]), w_main[l], row(b_main[l]),
                               w_gate[l], row(b_gate[l]))
        ym = _mlstm(main, gates, mlstm_conv_w[l], row(mlstm_conv_b[l]),
                    row(mlstm_head_norm[l]), batch, seq)
        ya = _fox(main, gates, row(fox_head_norm[l]), batch, seq)
        yc = _cconv(main, conv_dw_w[l], row(conv_dw_b[l]), row(conv_ln_g[l]),
                    row(conv_ln_b[l]), batch, seq)
        x2 = _out_proj(x2, ym, ya, yc, w_out_b[l])
        x2 = _ffn(x2, row(norm_ffn[l]), w_up_b[l], w_down_b[l], row(final_norm),
                  final_norm=(l == depth - 1))
    return x2.reshape(batch, seq, d)
```

```python
import functools
import math

import jax
import jax.numpy as jnp
import numpy as np
from jax import lax
from jax.experimental import pallas as pl
from jax.experimental.pallas import tpu as pltpu

D_MODEL = 2048
DEPTH = 4
HEAD_DIM = 128
D_MLSTM = D_MODEL // 2
D_FOX = D_MODEL // 4
D_CONV = D_MODEL - D_MLSTM - D_FOX
MLSTM_HEADS = D_MLSTM // HEAD_DIM
FOX_HEADS = D_FOX // HEAD_DIM
MLSTM_CONV_WIDTH = 4
CONV_WIDTH = 31
D_FF = 4 * D_MODEL
MLSTM_CHUNK = 128
EPS = 1e-6

LANES = 128
SUBLANES = 8
VMEM_LIMIT = 56 * 1024 * 1024

D_MAIN = 4 * D_MLSTM + 3 * D_FOX + 2 * D_CONV
GATE_I = 0
GATE_F = MLSTM_HEADS
GATE_FA = 2 * MLSTM_HEADS
BLK_QA, BLK_KA, BLK_VA, BLK_UC, BLK_GC = 8, 9, 10, 11, 12

BF16 = jnp.bfloat16
F32 = jnp.float32
NT_DIMS = (((1,), (1,)), ((), ()))
TN_DIMS = (((0,), (0,)), ((), ()))


def _rms_rows(x, g):
    ms = jnp.mean(x * x, axis=-1, keepdims=True)
    return x * lax.rsqrt(ms + EPS) * g


def _sigmoid(x):
    return 0.5 * jnp.tanh(0.5 * x) + 0.5


def _log_sigmoid(x):
    return jnp.minimum(x, 0.0) - jnp.log(1.0 + jnp.exp(-jnp.abs(x)))


def _swish(x, gain=1.0):
    return x * ((0.5 * gain) * jnp.tanh(0.5 * x) + (0.5 * gain))


def _causal_mask(n):
    r = lax.broadcasted_iota(jnp.int32, (n, n), 0)
    c = lax.broadcasted_iota(jnp.int32, (n, n), 1)
    return r >= c


def _split3(x):
    hi = x.astype(BF16)
    rem = x - hi.astype(F32)
    mid = rem.astype(BF16)
    lo = (rem - mid.astype(F32)).astype(BF16)
    return jnp.concatenate([hi, mid, lo], axis=-1)


def _cumsum_rows(x):
    n = x.shape[0]
    tri = jnp.where(_causal_mask(n), 1.0, 0.0).astype(BF16)
    y = jnp.dot(tri, _split3(x), preferred_element_type=F32)
    return y[:, :LANES] + y[:, LANES:2 * LANES] + y[:, 2 * LANES:]


IN_TM = 512
IN_TN = 512
IN_SLAB = 128


def _in_proj_kernel(x_ref, g_ref, w_ref, b_ref, wg_ref, bg_ref, out_ref, gates_ref, xn_ref):
    def slab(r, carry):
        rows = pl.ds(pl.multiple_of(r * IN_SLAB, IN_SLAB), IN_SLAB)
        xn = _rms_rows(x_ref[rows, :], g_ref[...]).astype(BF16)
        xn_ref[rows, :] = xn
        gates_ref[rows, :] = (
            jnp.dot(xn, wg_ref[...], preferred_element_type=F32) + bg_ref[...])
        return carry
    lax.fori_loop(0, IN_TM // IN_SLAB, slab, 0)

    for n in range(D_MAIN // IN_TN):
        cols = slice(n * IN_TN, (n + 1) * IN_TN)
        acc = jnp.dot(xn_ref[...], w_ref[:, cols], preferred_element_type=F32)
        out_ref[:, cols] = (acc + b_ref[:, cols]).astype(out_ref.dtype)


def _in_proj(x2, g, w_main, b_main, w_gate, b_gate):
    m = x2.shape[0]
    resident = pl.Buffered(1)
    return pl.pallas_call(
        _in_proj_kernel,
        grid=(m // IN_TM,),
        in_specs=[
            pl.BlockSpec((IN_TM, D_MODEL), lambda i: (i, 0)),
            pl.BlockSpec((1, D_MODEL), lambda i: (0, 0), pipeline_mode=resident),
            pl.BlockSpec((D_MODEL, D_MAIN), lambda i: (0, 0), pipeline_mode=resident),
            pl.BlockSpec((1, D_MAIN), lambda i: (0, 0), pipeline_mode=resident),
            pl.BlockSpec((D_MODEL, LANES), lambda i: (0, 0), pipeline_mode=resident),
            pl.BlockSpec((1, LANES), lambda i: (0, 0), pipeline_mode=resident),
        ],
        out_specs=[
            pl.BlockSpec((IN_TM, D_MAIN), lambda i: (i, 0)),
            pl.BlockSpec((IN_TM, LANES), lambda i: (i, 0)),
        ],
        out_shape=[
            jax.ShapeDtypeStruct((m, D_MAIN), BF16),
            jax.ShapeDtypeStruct((m, LANES), F32),
        ],
        scratch_shapes=[pltpu.VMEM((IN_TM, D_MODEL), BF16)],
        compiler_params=pltpu.CompilerParams(
            dimension_semantics=("parallel",),
            vmem_limit_bytes=VMEM_LIMIT),
        name="in_proj",
    )(x2, g, w_main, b_main, w_gate, b_gate)


CONV_PAD = SUBLANES


def _mlstm_kernel(q_ref, k_ref, v_ref, o_ref, gt_ref, wq_ref, wk_ref, bq_ref, bk_ref,
                  hn_ref, y_ref, qbuf, kbuf, c_ref, m_ref):
    L = MLSTM_CHUNK
    c = pl.program_id(1)

    @pl.when(c == 0)
    def _():
        qbuf[0:CONV_PAD, :] = jnp.zeros((CONV_PAD, D_MLSTM), F32)
        kbuf[0:CONV_PAD, :] = jnp.zeros((CONV_PAD, D_MLSTM), F32)
        c_ref[...] = jnp.zeros_like(c_ref)
        m_ref[...] = jnp.zeros_like(m_ref)

    @pl.when(c > 0)
    def _():
        qbuf[0:CONV_PAD, :] = qbuf[L:L + CONV_PAD, :]
        kbuf[0:CONV_PAD, :] = kbuf[L:L + CONV_PAD, :]

    qbuf[CONV_PAD:CONV_PAD + L, :] = q_ref[...].astype(F32)
    kbuf[CONV_PAD:CONV_PAD + L, :] = k_ref[...].astype(F32)

    gates = gt_ref[...]
    bcum = _cumsum_rows(_log_sigmoid(gates))
    gates_t = gates.T
    bcum_t = bcum.T
    causal = _causal_mask(L)
    ones_ext = jnp.ones((L, HEAD_DIM), BF16)

    def conv_swish(buf, w_ref, b_ref, hs, gain):
        last = MLSTM_CONV_WIDTH - 1
        acc = b_ref[:, hs] + w_ref[last:last + 1, hs] * buf[CONV_PAD:CONV_PAD + L, hs]
        for j in range(last):
            off = CONV_PAD - last + j
            acc = acc + w_ref[j:j + 1, hs] * buf[off:off + L, hs]
        return _swish(acc, gain)

    for h in range(MLSTM_HEADS):
        hs = slice(h * HEAD_DIM, (h + 1) * HEAD_DIM)
        b_c = jnp.broadcast_to(bcum[:, GATE_F + h:GATE_F + h + 1], (L, L))
        i_c = jnp.broadcast_to(gates[:, GATE_I + h:GATE_I + h + 1], (L, L))
        b_r = bcum_t[GATE_F + h:GATE_F + h + 1, :]
        i_r = gates_t[GATE_I + h:GATE_I + h + 1, :]
        m_prev = m_ref[h, 0:1, :]

        d_log = jnp.where(causal, b_c - b_r + i_r, -jnp.inf)
        inter = b_c + m_prev
        m_t = jnp.maximum(inter, jnp.max(d_log, axis=-1, keepdims=True))
        w_intra = jnp.exp(d_log - m_t)
        w_inter = jnp.exp(inter - m_t)

        qh = conv_swish(qbuf, wq_ref, bq_ref, hs, 1.0)
        kh = conv_swish(kbuf, wk_ref, bk_ref, hs, HEAD_DIM ** -0.5)
        qb = qh.astype(BF16)
        kb = kh.astype(BF16)
        v_ext = jnp.concatenate([v_ref[:, hs], ones_ext], axis=-1)
        c_ext = c_ref[h]

        s = lax.dot_general(qb, kb, NT_DIMS, preferred_element_type=F32) * w_intra
        qc = jnp.dot(qb, c_ext.astype(BF16), preferred_element_type=F32)
        sv = jnp.dot(s.astype(BF16), v_ext, preferred_element_type=F32)
        nx = jnp.concatenate([w_inter, w_inter], axis=-1) * qc + sv
        num = nx[:, :HEAD_DIM]
        den = nx[:, HEAD_DIM:]
        hh = num / jnp.maximum(jnp.abs(den), jnp.exp(-m_t))

        b_last = b_c[L - 1:L, :]
        g_c = b_last - b_c + i_c
        m_new = jnp.maximum(b_last + m_prev, jnp.max(g_c, axis=0, keepdims=True))
        w_k = jnp.exp(g_c - m_new)
        decay = jnp.exp(b_last + m_prev - m_new)
        kw = (kh * w_k).astype(BF16)
        upd = lax.dot_general(kw, v_ext, TN_DIMS, preferred_element_type=F32)
        c_ref[h] = jnp.concatenate([decay, decay], axis=-1) * c_ext + upd
        m_ref[h] = jnp.broadcast_to(m_new, (SUBLANES, LANES))

        hn = _rms_rows(hh, hn_ref[:, hs])
        y_ref[:, hs] = (hn * _sigmoid(o_ref[:, hs].astype(F32))).astype(y_ref.dtype)


def _mlstm(main, gates, conv_w, conv_b, head_norm, batch, seq):
    L = MLSTM_CHUNK
    nc = seq // L
    m = main.shape[0]
    row = lambda b, c: b * nc + c
    return pl.pallas_call(
        _mlstm_kernel,
        grid=(batch, nc),
        in_specs=[
            pl.BlockSpec((L, D_MLSTM), lambda b, c: (row(b, c), 0)),
            pl.BlockSpec((L, D_MLSTM), lambda b, c: (row(b, c), 1)),
            pl.BlockSpec((L, D_MLSTM), lambda b, c: (row(b, c), 2)),
            pl.BlockSpec((L, D_MLSTM), lambda b, c: (row(b, c), 3)),
            pl.BlockSpec((L, LANES), lambda b, c: (row(b, c), 0)),
            pl.BlockSpec((MLSTM_CONV_WIDTH, D_MLSTM), lambda b, c: (0, 0)),
            pl.BlockSpec((MLSTM_CONV_WIDTH, D_MLSTM), lambda b, c: (0, 1)),
            pl.BlockSpec((1, D_MLSTM), lambda b, c: (0, 0)),
            pl.BlockSpec((1, D_MLSTM), lambda b, c: (0, 1)),
            pl.BlockSpec((1, D_MLSTM), lambda b, c: (0, 0)),
        ],
        out_specs=pl.BlockSpec((L, D_MLSTM), lambda b, c: (row(b, c), 0)),
        out_shape=jax.ShapeDtypeStruct((m, D_MLSTM), BF16),
        scratch_shapes=[
            pltpu.VMEM((CONV_PAD + L, D_MLSTM), F32),
            pltpu.VMEM((CONV_PAD + L, D_MLSTM), F32),
            pltpu.VMEM((MLSTM_HEADS, HEAD_DIM, 2 * HEAD_DIM), F32),
            pltpu.VMEM((MLSTM_HEADS, SUBLANES, LANES), F32),
        ],
        compiler_params=pltpu.CompilerParams(
            dimension_semantics=("parallel", "arbitrary"),
            vmem_limit_bytes=VMEM_LIMIT),
        name="mlstm",
    )(main, main, main, main, gates, conv_w, conv_w, conv_b, conv_b, head_norm)


FOX_TQ = 256
FOX_TK = 256
FOX_SPLIT = 3
FOX_EXP2_SCALE = (HEAD_DIM ** -0.5) * math.log2(math.e)


def _fox_bias_selectors():
    sel = np.zeros((FOX_SPLIT * LANES, 2 * D_FOX), np.float32)
    for h in range(FOX_HEADS):
        for p in range(FOX_SPLIT):
            sel[p * LANES + GATE_FA + h, h * HEAD_DIM + p] = 1.0
            sel[p * LANES + GATE_FA + h, D_FOX + h * HEAD_DIM + FOX_SPLIT + p] = -1.0
    return jnp.asarray(sel, BF16)


def _fox_kernel(q_ref, k_ref, v_ref, gt_ref, sel_ref, hn_ref, y_ref, qbias, kbias, u_ref, acc_ref,
                *, seq):
    qi = pl.program_id(1)
    heads = [slice(h * HEAD_DIM, (h + 1) * HEAD_DIM) for h in range(FOX_HEADS)]

    @pl.when(qi == 0)
    def _():
        lane = lax.broadcasted_iota(jnp.int32, (1, 2 * D_FOX), 1)
        in_head = lane % HEAD_DIM
        ones_k = jnp.where(in_head < FOX_SPLIT, 1.0, 0.0)
        ones_q = jnp.where(in_head < 2 * FOX_SPLIT, 1.0, 0.0) - ones_k
        ones = jnp.where(lane >= D_FOX, ones_k, ones_q)
        carry = jnp.zeros((1, LANES), F32)
        for blk in range(seq // FOX_TK):
            rows = slice(blk * FOX_TK, (blk + 1) * FOX_TK)
            cs = _cumsum_rows(_log_sigmoid(gt_ref[rows, :])) + carry
            carry = cs[FOX_TK - 1:FOX_TK, :]
            bias = jnp.dot(_split3(cs * (HEAD_DIM ** 0.5)), sel_ref[...],
                           preferred_element_type=F32) + ones
            qbias[rows, :] = bias[:, :D_FOX].astype(BF16)
            kbias[rows, :] = bias[:, D_FOX:].astype(BF16)

    q_rows = pl.ds(pl.multiple_of(qi * FOX_TQ, FOX_TQ), FOX_TQ)
    q_aug = [jnp.concatenate([q_ref[:, hs], qbias[q_rows, hs]], axis=-1) for hs in heads]
    ones_ext = jnp.ones((FOX_TK, HEAD_DIM), BF16)
    acc_ref[...] = jnp.zeros_like(acc_ref)

    def logits_into(kb, slot):
        ks = pl.ds(pl.multiple_of(kb * FOX_TK, FOX_TK), FOX_TK)
        for h, hs in enumerate(heads):
            k_aug = jnp.concatenate([k_ref[ks, hs], kbias[ks, hs]], axis=-1)
            u_ref[slot, h] = lax.dot_general(q_aug[h], k_aug, NT_DIMS, preferred_element_type=F32)

    def consume(kb, slot, m_old, diagonal):
        ks = pl.ds(pl.multiple_of(kb * FOX_TK, FOX_TK), FOX_TK)
        m_out = []
        for h, hs in enumerate(heads):
            u = u_ref[slot, h]
            if diagonal:
                u = jnp.where(_causal_mask(FOX_TQ), u, -jnp.inf)
            m_new = jnp.maximum(m_old[h], jnp.max(u, axis=-1, keepdims=True))
            p = jnp.exp2((u - m_new) * FOX_EXP2_SCALE)
            alpha = jnp.exp2((m_old[h] - m_new) * FOX_EXP2_SCALE)
            v_ext = jnp.concatenate([v_ref[ks, hs], ones_ext], axis=-1)
            acc_ref[h] = alpha * acc_ref[h] + jnp.dot(p.astype(BF16), v_ext,
                                                      preferred_element_type=F32)
            m_out.append(m_new)
        return tuple(m_out)

    def body(kb, m_old):
        slot = kb % 2
        m_new = consume(kb, slot, m_old, False)
        logits_into(kb + 1, 1 - slot)
        return m_new

    logits_into(0, 0)
    m_init = tuple(jnp.full((FOX_TQ, 1), -jnp.inf, F32) for _ in heads)
    m_run = lax.fori_loop(0, qi, body, m_init)
    consume(qi, qi % 2, m_run, True)

    for h, hs in enumerate(heads):
        acc = acc_ref[h]
        o = acc[:, :HEAD_DIM] / acc[:, HEAD_DIM:]
        y_ref[:, hs] = _rms_rows(o, hn_ref[:, hs]).astype(y_ref.dtype)


def _fox(main, gates, head_norm, batch, seq):
    nq = seq // FOX_TQ
    m = main.shape[0]
    sel = _fox_bias_selectors()
    return pl.pallas_call(
        functools.partial(_fox_kernel, seq=seq),
        grid=(batch, nq),
        in_specs=[
            pl.BlockSpec((FOX_TQ, D_FOX), lambda b, i: (b * nq + i, BLK_QA)),
            pl.BlockSpec((seq, D_FOX), lambda b, i: (b, BLK_KA)),
            pl.BlockSpec((seq, D_FOX), lambda b, i: (b, BLK_VA)),
            pl.BlockSpec((seq, LANES), lambda b, i: (b, 0)),
            pl.BlockSpec(sel.shape, lambda b, i: (0, 0)),
            pl.BlockSpec((1, D_FOX), lambda b, i: (0, 0)),
        ],
        out_specs=pl.BlockSpec((FOX_TQ, D_FOX), lambda b, i: (b * nq + i, 0)),
        out_shape=jax.ShapeDtypeStruct((m, D_FOX), BF16),
        scratch_shapes=[
            pltpu.VMEM((seq, D_FOX), BF16),
            pltpu.VMEM((seq, D_FOX), BF16),
            pltpu.VMEM((2, FOX_HEADS, FOX_TQ, FOX_TK), F32),
            pltpu.VMEM((FOX_HEADS, FOX_TQ, 2 * HEAD_DIM), F32),
        ],
        compiler_params=pltpu.CompilerParams(
            dimension_semantics=("parallel", "arbitrary"),
            vmem_limit_bytes=VMEM_LIMIT),
        name="fox",
    )(main, main, main, gates, sel, head_norm)


CC_TS = 256
CC_HIST = 32
CC_ROWS = 32
CC_SHIFT_ROWS = CC_HIST + CC_TS - SUBLANES


def _cconv_kernel(u_ref, g_ref, w_ref, b_ref, lg_ref, lb_ref, y_ref, ybuf, yshift):
    si = pl.program_id(1)

    @pl.when(si == 0)
    def _():
        ybuf[0:CC_HIST, :] = jnp.zeros((CC_HIST, D_CONV), F32)

    @pl.when(si > 0)
    def _():
        ybuf[0:CC_HIST, :] = ybuf[CC_TS:CC_TS + CC_HIST, :]

    ybuf[CC_HIST:CC_HIST + CC_TS, :] = (
        u_ref[...].astype(F32) * _sigmoid(g_ref[...].astype(F32)))

    first = CC_HIST - (CONV_WIDTH - 1)
    for phase in range(1, SUBLANES):
        yshift[phase - 1] = ybuf[phase:phase + CC_SHIFT_ROWS, :]
    for r in range(CC_TS // CC_ROWS):
        acc = jnp.broadcast_to(b_ref[...], (CC_ROWS, D_CONV))
        for j in range(CONV_WIDTH):
            phase = (first + j) % SUBLANES
            lo = r * CC_ROWS + (first + j) - phase
            if phase == 0:
                tap = ybuf[lo:lo + CC_ROWS, :]
            else:
                tap = yshift[phase - 1, lo:lo + CC_ROWS, :]
            acc = acc + w_ref[j:j + 1, :] * tap
        mu = jnp.mean(acc, axis=-1, keepdims=True)
        xc = acc - mu
        yn = xc * lax.rsqrt(jnp.mean(xc * xc, axis=-1, keepdims=True) + EPS)
        yn = yn * lg_ref[...] + lb_ref[...]
        y_ref[r * CC_ROWS:(r + 1) * CC_ROWS, :] = _swish(yn).astype(y_ref.dtype)


def _cconv(main, w, b, ln_g, ln_b, batch, seq):
    ns = seq // CC_TS
    m = main.shape[0]
    return pl.pallas_call(
        _cconv_kernel,
        grid=(batch, ns),
        in_specs=[
            pl.BlockSpec((CC_TS, D_CONV), lambda bi, s: (bi * ns + s, BLK_UC)),
            pl.BlockSpec((CC_TS, D_CONV), lambda bi, s: (bi * ns + s, BLK_GC)),
            pl.BlockSpec((CONV_WIDTH, D_CONV), lambda bi, s: (0, 0)),
            pl.BlockSpec((1, D_CONV), lambda bi, s: (0, 0)),
            pl.BlockSpec((1, D_CONV), lambda bi, s: (0, 0)),
            pl.BlockSpec((1, D_CONV), lambda bi, s: (0, 0)),
        ],
        out_specs=pl.BlockSpec((CC_TS, D_CONV), lambda bi, s: (bi * ns + s, 0)),
        out_shape=jax.ShapeDtypeStruct((m, D_CONV), BF16),
        scratch_shapes=[
            pltpu.VMEM((CC_HIST + CC_TS, D_CONV), F32),
            pltpu.VMEM((SUBLANES - 1, CC_SHIFT_ROWS, D_CONV), F32),
        ],
        compiler_params=pltpu.CompilerParams(
            dimension_semantics=("parallel", "arbitrary"),
            vmem_limit_bytes=VMEM_LIMIT),
        name="cconv",
    )(main, main, w, b, ln_g, ln_b)


OUT_TM = 1024
OUT_TN = 1024


def _out_proj_kernel(x_ref, ym_ref, ya_ref, yc_ref, wm_ref, wa_ref, wc_ref, o_ref):
    acc = jnp.dot(ym_ref[...], wm_ref[...], preferred_element_type=F32)
    acc = acc + jnp.dot(ya_ref[...], wa_ref[...], preferred_element_type=F32)
    acc = acc + jnp.dot(yc_ref[...], wc_ref[...], preferred_element_type=F32)
    o_ref[...] = x_ref[...] + acc


def _out_proj(x2, ym, ya, yc, w_out):
    m = x2.shape[0]
    fox_blk = D_MLSTM // D_FOX
    return pl.pallas_call(
        _out_proj_kernel,
        grid=(m // OUT_TM, D_MODEL // OUT_TN),
        in_specs=[
            pl.BlockSpec((OUT_TM, OUT_TN), lambda i, j: (i, j)),
            pl.BlockSpec((OUT_TM, D_MLSTM), lambda i, j: (i, 0)),
            pl.BlockSpec((OUT_TM, D_FOX), lambda i, j: (i, 0)),
            pl.BlockSpec((OUT_TM, D_CONV), lambda i, j: (i, 0)),
            pl.BlockSpec((D_MLSTM, OUT_TN), lambda i, j: (0, j)),
            pl.BlockSpec((D_FOX, OUT_TN), lambda i, j: (fox_blk, j)),
            pl.BlockSpec((D_CONV, OUT_TN), lambda i, j: (fox_blk + 1, j)),
        ],
        out_specs=pl.BlockSpec((OUT_TM, OUT_TN), lambda i, j: (i, j)),
        out_shape=jax.ShapeDtypeStruct((m, D_MODEL), F32),
        compiler_params=pltpu.CompilerParams(
            dimension_semantics=("parallel", "arbitrary"),
            vmem_limit_bytes=VMEM_LIMIT),
        name="out_proj",
    )(x2, ym, ya, yc, w_out, w_out, w_out)


FFN_TM = 1024
FFN_TF = 512
FFN_SLAB = 128


def _ffn_kernel(x_ref, g_ref, wu_ref, wd_ref, fg_ref, o_ref, xn_ref, *, final_norm):
    j = pl.program_id(1)

    @pl.when(j == 0)
    def _():
        def slab(r, carry):
            rows = pl.ds(pl.multiple_of(r * FFN_SLAB, FFN_SLAB), FFN_SLAB)
            x = x_ref[rows, :]
            xn_ref[rows, :] = _rms_rows(x, g_ref[...]).astype(BF16)
            o_ref[rows, :] = x
            return carry
        lax.fori_loop(0, FFN_TM // FFN_SLAB, slab, 0)

    hid = jnp.maximum(jnp.dot(xn_ref[...], wu_ref[...], preferred_element_type=F32), 0.0)
    hid = (hid * hid).astype(BF16)
    o_ref[...] += jnp.dot(hid, wd_ref[...], preferred_element_type=F32)

    if final_norm:
        @pl.when(j == pl.num_programs(1) - 1)
        def _():
            def slab(r, carry):
                rows = pl.ds(pl.multiple_of(r * FFN_SLAB, FFN_SLAB), FFN_SLAB)
                o_ref[rows, :] = _rms_rows(o_ref[rows, :], fg_ref[...])
                return carry
            lax.fori_loop(0, FFN_TM // FFN_SLAB, slab, 0)


def _ffn(x2, g, w_up, w_down, final_g, final_norm):
    m = x2.shape[0]
    return pl.pallas_call(
        functools.partial(_ffn_kernel, final_norm=final_norm),
        grid=(m // FFN_TM, D_FF // FFN_TF),
        in_specs=[
            pl.BlockSpec((FFN_TM, D_MODEL), lambda i, j: (i, 0)),
            pl.BlockSpec((1, D_MODEL), lambda i, j: (0, 0)),
            pl.BlockSpec((D_MODEL, FFN_TF), lambda i, j: (0, j)),
            pl.BlockSpec((FFN_TF, D_MODEL), lambda i, j: (j, 0)),
            pl.BlockSpec((1, D_MODEL), lambda i, j: (0, 0)),
        ],
        out_specs=pl.BlockSpec((FFN_TM, D_MODEL), lambda i, j: (i, 0)),
        out_shape=jax.ShapeDtypeStruct((m, D_MODEL), F32),
        scratch_shapes=[pltpu.VMEM((FFN_TM, D_MODEL), BF16)],
        compiler_params=pltpu.CompilerParams(
            dimension_semantics=("parallel", "arbitrary"),
            vmem_limit_bytes=VMEM_LIMIT),
        name="ffn",
    )(x2, g, w_up, w_down, final_g)


OFF_GATE_M = 4 * D_MLSTM
OFF_FOX = OFF_GATE_M + 2 * MLSTM_HEADS
OFF_GATE_A = OFF_FOX + 3 * D_FOX
OFF_CONV = OFF_GATE_A + FOX_HEADS
D_IN = OFF_CONV + 2 * D_CONV
MAIN_FOX = 4 * D_MLSTM
MAIN_CONV = MAIN_FOX + 3 * D_FOX
assert OFF_GATE_M % LANES == 0 and (OFF_GATE_A - GATE_FA) % LANES == 0
PREP_ROWS = 256


def _prep_in_kernel(w_ref, main_ref, gate_ref):
    main_ref[:, 0:MAIN_FOX] = w_ref[:, 0:OFF_GATE_M].astype(BF16)
    main_ref[:, MAIN_FOX:MAIN_CONV] = w_ref[:, OFF_FOX:OFF_GATE_A].astype(BF16)
    main_ref[:, MAIN_CONV:D_MAIN] = w_ref[:, OFF_CONV:D_IN].astype(BF16)
    lane = lax.broadcasted_iota(jnp.int32, (1, LANES), 1)
    gate_m = w_ref[:, OFF_GATE_M:OFF_GATE_M + LANES]
    gate_a = w_ref[:, OFF_GATE_A - GATE_FA:OFF_GATE_A - GATE_FA + LANES]
    gate = jnp.where(lane < GATE_FA, gate_m, jnp.where(lane < GATE_FA + FOX_HEADS, gate_a, 0.0))
    gate_ref[...] = gate.astype(BF16)


def _prep_in_proj(w_in):
    depth, d, _ = w_in.shape
    return pl.pallas_call(
        _prep_in_kernel,
        grid=(depth, d // PREP_ROWS),
        in_specs=[pl.BlockSpec((None, PREP_ROWS, D_IN), lambda l, i: (l, i, 0))],
        out_specs=[
            pl.BlockSpec((None, PREP_ROWS, D_MAIN), lambda l, i: (l, i, 0)),
            pl.BlockSpec((None, PREP_ROWS, LANES), lambda l, i: (l, i, 0)),
        ],
        out_shape=[
            jax.ShapeDtypeStruct((depth, d, D_MAIN), BF16),
            jax.ShapeDtypeStruct((depth, d, LANES), BF16),
        ],
        compiler_params=pltpu.CompilerParams(
            dimension_semantics=("parallel", "parallel"),
            vmem_limit_bytes=VMEM_LIMIT),
        name="prep_in_proj",
    )(w_in)


def _split_bias(b_in):
    main = jnp.concatenate([b_in[..., 0:OFF_GATE_M], b_in[..., OFF_FOX:OFF_GATE_A],
                            b_in[..., OFF_CONV:D_IN]], axis=-1)
    pad = LANES - GATE_FA - FOX_HEADS
    gate = jnp.concatenate([b_in[..., OFF_GATE_M:OFF_FOX], b_in[..., OFF_GATE_A:OFF_CONV],
                            jnp.zeros(b_in.shape[:-1] + (pad,), b_in.dtype)], axis=-1)
    return main, gate


def kernel(x, norm_mix, w_in, b_in, mlstm_conv_w, mlstm_conv_b, mlstm_head_norm, fox_head_norm,
           conv_dw_w, conv_dw_b, conv_ln_g, conv_ln_b, w_out, norm_ffn, w_up, w_down, final_norm):
    batch, seq, d = x.shape
    assert d == D_MODEL and seq % FOX_TQ == 0 and (batch * seq) % IN_TM == 0
    depth = w_in.shape[0]
    x2 = x.reshape(batch * seq, d)
    assert w_in.shape[1:] == (D_MODEL, D_IN)
    w_main, w_gate = _prep_in_proj(w_in)
    b_main, b_gate = _split_bias(b_in)
    w_out_b = w_out.astype(BF16)
    w_up_b = w_up.astype(BF16)
    w_down_b = w_down.astype(BF16)
    row = lambda a: a.reshape(1, -1)

    for l in range(depth):
        main, gates = _in_proj(x2, row(norm_mix[l]), w_main[l], row(b_main[l]),
                               w_gate[l], row(b_gate[l]))
        ym = _mlstm(main, gates, mlstm_conv_w[l], row(mlstm_conv_b[l]),
                    row(mlstm_head_norm[l]), batch, seq)
        ya = _fox(main, gates, row(fox_head_norm[l]), batch, seq)
        yc = _cconv(main, conv_dw_w[l], row(conv_dw_b[l]), row(conv_ln_g[l]),
                    row(conv_ln_b[l]), batch, seq)
        x2 = _out_proj(x2, ym, ya, yc, w_out_b[l])
        x2 = _ffn(x2, row(norm_ffn[l]), w_up_b[l], w_down_b[l], row(final_norm),
                  final_norm=(l == depth - 1))
    return x2.reshape(batch, seq, d)
```

```python
import functools
import math

import jax
import jax.numpy as jnp
import numpy as np
from jax import lax
from jax.experimental import pallas as pl
from jax.experimental.pallas import tpu as pltpu

D_MODEL = 2048
DEPTH = 4
HEAD_DIM = 128
D_MLSTM = D_MODEL // 2
D_FOX = D_MODEL // 4
D_CONV = D_MODEL - D_MLSTM - D_FOX
MLSTM_HEADS = D_MLSTM // HEAD_DIM
FOX_HEADS = D_FOX // HEAD_DIM
MLSTM_CONV_WIDTH = 4
CONV_WIDTH = 31
D_FF = 4 * D_MODEL
MLSTM_CHUNK = 128
EPS = 1e-6

LANES = 128
SUBLANES = 8
VMEM_LIMIT = 56 * 1024 * 1024

D_MAIN = 4 * D_MLSTM + 3 * D_FOX + 2 * D_CONV
GATE_I = 0
GATE_F = MLSTM_HEADS
GATE_FA = 2 * MLSTM_HEADS
BLK_QA, BLK_KA, BLK_VA, BLK_UC, BLK_GC = 8, 9, 10, 11, 12

BF16 = jnp.bfloat16
F32 = jnp.float32
NT_DIMS = (((1,), (1,)), ((), ()))
TN_DIMS = (((0,), (0,)), ((), ()))


def _rms_rows(x, g):
    ms = jnp.mean(x * x, axis=-1, keepdims=True)
    return x * lax.rsqrt(ms + EPS) * g


def _sigmoid(x):
    return 0.5 * jnp.tanh(0.5 * x) + 0.5


def _log_sigmoid(x):
    return jnp.minimum(x, 0.0) - jnp.log(1.0 + jnp.exp(-jnp.abs(x)))


def _swish(x, gain=1.0):
    return x * ((0.5 * gain) * jnp.tanh(0.5 * x) + (0.5 * gain))


def _causal_mask(n):
    r = lax.broadcasted_iota(jnp.int32, (n, n), 0)
    c = lax.broadcasted_iota(jnp.int32, (n, n), 1)
    return r >= c


def _split3(x):
    hi = x.astype(BF16)
    rem = x - hi.astype(F32)
    mid = rem.astype(BF16)
    lo = (rem - mid.astype(F32)).astype(BF16)
    return jnp.concatenate([hi, mid, lo], axis=-1)


def _cumsum_rows(x):
    n = x.shape[0]
    tri = jnp.where(_causal_mask(n), 1.0, 0.0).astype(BF16)
    y = jnp.dot(tri, _split3(x), preferred_element_type=F32)
    return y[:, :LANES] + y[:, LANES:2 * LANES] + y[:, 2 * LANES:]


IN_TM = 512
IN_TN = 512
IN_SLAB = 128


def _in_proj_kernel(x_ref, g_ref, w_ref, b_ref, wg_ref, bg_ref, out_ref, gates_ref, xn_ref):
    def slab(r, carry):
        rows = pl.ds(pl.multiple_of(r * IN_SLAB, IN_SLAB), IN_SLAB)
        xn = _rms_rows(x_ref[rows, :], g_ref[...]).astype(BF16)
        xn_ref[rows, :] = xn
        gates_ref[rows, :] = (
            jnp.dot(xn, wg_ref[...], preferred_element_type=F32) + bg_ref[...])
        return carry
    lax.fori_loop(0, IN_TM // IN_SLAB, slab, 0)

    for n in range(D_MAIN // IN_TN):
        cols = slice(n * IN_TN, (n + 1) * IN_TN)
        acc = jnp.dot(xn_ref[...], w_ref[:, cols], preferred_element_type=F32)
        out_ref[:, cols] = (acc + b_ref[:, cols]).astype(out_ref.dtype)


def _in_proj(x2, g, w_main, b_main, w_gate, b_gate, layer):
    m = x2.shape[0]
    resident = pl.Buffered(1)
    return pl.pallas_call(
        _in_proj_kernel,
        grid=(m // IN_TM,),
        in_specs=[
            pl.BlockSpec((IN_TM, D_MODEL), lambda i: (i, 0)),
            pl.BlockSpec((1, D_MODEL), lambda i: (0, 0), pipeline_mode=resident),
            pl.BlockSpec((None, D_MODEL, D_MAIN), lambda i: (layer, 0, 0), pipeline_mode=resident),
            pl.BlockSpec((1, D_MAIN), lambda i: (0, 0), pipeline_mode=resident),
            pl.BlockSpec((None, D_MODEL, LANES), lambda i: (layer, 0, 0), pipeline_mode=resident),
            pl.BlockSpec((1, LANES), lambda i: (0, 0), pipeline_mode=resident),
        ],
        out_specs=[
            pl.BlockSpec((IN_TM, D_MAIN), lambda i: (i, 0)),
            pl.BlockSpec((IN_TM, LANES), lambda i: (i, 0)),
        ],
        out_shape=[
            jax.ShapeDtypeStruct((m, D_MAIN), BF16),
            jax.ShapeDtypeStruct((m, LANES), F32),
        ],
        scratch_shapes=[pltpu.VMEM((IN_TM, D_MODEL), BF16)],
        compiler_params=pltpu.CompilerParams(
            dimension_semantics=("parallel",),
            vmem_limit_bytes=VMEM_LIMIT),
        name="in_proj",
    )(x2, g, w_main, b_main, w_gate, b_gate)


CONV_PAD = SUBLANES


def _gate_prep(gates):
    bcum = _cumsum_rows(_log_sigmoid(gates))
    return gates, bcum, gates.T, bcum.T


def _mlstm_kernel(q_ref, k_ref, v_ref, o_ref, gt_ref, gt_next_ref, wq_ref, wk_ref, bq_ref, bk_ref,
                  hn_ref, y_ref, qbuf, kbuf, c_ref, m_ref, prep_ref):
    L = MLSTM_CHUNK
    c = pl.program_id(1)
    step = pl.program_id(0) * pl.num_programs(1) + c
    slot = step % 2

    @pl.when(step == 0)
    def _():
        for idx, tile in enumerate(_gate_prep(gt_ref[...])):
            prep_ref[0, idx] = tile

    @pl.when(c == 0)
    def _():
        qbuf[0:CONV_PAD, :] = jnp.zeros((CONV_PAD, D_MLSTM), F32)
        kbuf[0:CONV_PAD, :] = jnp.zeros((CONV_PAD, D_MLSTM), F32)
        c_ref[...] = jnp.zeros_like(c_ref)
        m_ref[...] = jnp.zeros_like(m_ref)

    @pl.when(c > 0)
    def _():
        qbuf[0:CONV_PAD, :] = qbuf[L:L + CONV_PAD, :]
        kbuf[0:CONV_PAD, :] = kbuf[L:L + CONV_PAD, :]

    qbuf[CONV_PAD:CONV_PAD + L, :] = q_ref[...].astype(F32)
    kbuf[CONV_PAD:CONV_PAD + L, :] = k_ref[...].astype(F32)

    gates = prep_ref[slot, 0]
    bcum = prep_ref[slot, 1]
    gates_t = prep_ref[slot, 2]
    bcum_t = prep_ref[slot, 3]
    causal = _causal_mask(L)
    ones_ext = jnp.ones((L, HEAD_DIM), BF16)

    def conv_swish(buf, w_ref, b_ref, hs, gain):
        last = MLSTM_CONV_WIDTH - 1
        acc = b_ref[:, hs] + w_ref[last:last + 1, hs] * buf[CONV_PAD:CONV_PAD + L, hs]
        for j in range(last):
            off = CONV_PAD - last + j
            acc = acc + w_ref[j:j + 1, hs] * buf[off:off + L, hs]
        return _swish(acc, gain)

    for h in range(MLSTM_HEADS):
        hs = slice(h * HEAD_DIM, (h + 1) * HEAD_DIM)
        b_c = jnp.broadcast_to(bcum[:, GATE_F + h:GATE_F + h + 1], (L, L))
        i_c = jnp.broadcast_to(gates[:, GATE_I + h:GATE_I + h + 1], (L, L))
        b_r = bcum_t[GATE_F + h:GATE_F + h + 1, :]
        i_r = gates_t[GATE_I + h:GATE_I + h + 1, :]
        m_prev = m_ref[h, 0:1, :]

        d_log = jnp.where(causal, b_c - b_r + i_r, -jnp.inf)
        inter = b_c + m_prev
        m_t = jnp.maximum(inter, jnp.max(d_log, axis=-1, keepdims=True))
        w_intra = jnp.exp(d_log - m_t)
        w_inter = jnp.exp(inter - m_t)

        qh = conv_swish(qbuf, wq_ref, bq_ref, hs, 1.0)
        kh = conv_swish(kbuf, wk_ref, bk_ref, hs, HEAD_DIM ** -0.5)
        qb = qh.astype(BF16)
        kb = kh.astype(BF16)
        v_ext = jnp.concatenate([v_ref[:, hs], ones_ext], axis=-1)
        c_ext = c_ref[h]

        s = lax.dot_general(qb, kb, NT_DIMS, preferred_element_type=F32) * w_intra
        qc = jnp.dot(qb, c_ext.astype(BF16), preferred_element_type=F32)
        sv = jnp.dot(s.astype(BF16), v_ext, preferred_element_type=F32)
        nx = jnp.concatenate([w_inter, w_inter], axis=-1) * qc + sv
        num = nx[:, :HEAD_DIM]
        den = nx[:, HEAD_DIM:]
        hh = num / jnp.maximum(jnp.abs(den), jnp.exp(-m_t))

        b_last = b_c[L - 1:L, :]
        g_c = b_last - b_c + i_c
        m_new = jnp.maximum(b_last + m_prev, jnp.max(g_c, axis=0, keepdims=True))
        w_k = jnp.exp(g_c - m_new)
        decay = jnp.exp(b_last + m_prev - m_new)
        kw = (kh * w_k).astype(BF16)
        upd = lax.dot_general(kw, v_ext, TN_DIMS, preferred_element_type=F32)
        c_ref[h] = jnp.concatenate([decay, decay], axis=-1) * c_ext + upd
        m_ref[h] = jnp.broadcast_to(m_new, (SUBLANES, LANES))

        hn = _rms_rows(hh, hn_ref[:, hs])
        y_ref[:, hs] = (hn * _sigmoid(o_ref[:, hs].astype(F32))).astype(y_ref.dtype)

    for idx, tile in enumerate(_gate_prep(gt_next_ref[...])):
        prep_ref[1 - slot, idx] = tile


def _mlstm(main, gates, conv_w, conv_b, head_norm, batch, seq):
    L = MLSTM_CHUNK
    nc = seq // L
    m = main.shape[0]
    row = lambda b, c: b * nc + c
    return pl.pallas_call(
        _mlstm_kernel,
        grid=(batch, nc),
        in_specs=[
            pl.BlockSpec((L, D_MLSTM), lambda b, c: (row(b, c), 0)),
            pl.BlockSpec((L, D_MLSTM), lambda b, c: (row(b, c), 1)),
            pl.BlockSpec((L, D_MLSTM), lambda b, c: (row(b, c), 2)),
            pl.BlockSpec((L, D_MLSTM), lambda b, c: (row(b, c), 3)),
            pl.BlockSpec((L, LANES), lambda b, c: (row(b, c), 0)),
            pl.BlockSpec((L, LANES), lambda b, c: (jnp.minimum(row(b, c) + 1, m // L - 1), 0)),
            pl.BlockSpec((MLSTM_CONV_WIDTH, D_MLSTM), lambda b, c: (0, 0)),
            pl.BlockSpec((MLSTM_CONV_WIDTH, D_MLSTM), lambda b, c: (0, 1)),
            pl.BlockSpec((1, D_MLSTM), lambda b, c: (0, 0)),
            pl.BlockSpec((1, D_MLSTM), lambda b, c: (0, 1)),
            pl.BlockSpec((1, D_MLSTM), lambda b, c: (0, 0)),
        ],
        out_specs=pl.BlockSpec((L, D_MLSTM), lambda b, c: (row(b, c), 0)),
        out_shape=jax.ShapeDtypeStruct((m, D_MLSTM), BF16),
        scratch_shapes=[
            pltpu.VMEM((CONV_PAD + L, D_MLSTM), F32),
            pltpu.VMEM((CONV_PAD + L, D_MLSTM), F32),
            pltpu.VMEM((MLSTM_HEADS, HEAD_DIM, 2 * HEAD_DIM), F32),
            pltpu.VMEM((MLSTM_HEADS, SUBLANES, LANES), F32),
            pltpu.VMEM((2, 4, L, LANES), F32),
        ],
        compiler_params=pltpu.CompilerParams(
            dimension_semantics=("arbitrary", "arbitrary"),
            vmem_limit_bytes=VMEM_LIMIT),
        name="mlstm",
    )(main, main, main, main, gates, gates, conv_w, conv_w, conv_b, conv_b, head_norm)


FOX_TQ = 256
FOX_TK = 256
FOX_SPLIT = 3
FOX_EXP2_SCALE = (HEAD_DIM ** -0.5) * math.log2(math.e)


def _fox_bias_selectors():
    sel = np.zeros((FOX_SPLIT * LANES, 2 * D_FOX), np.float32)
    for h in range(FOX_HEADS):
        for p in range(FOX_SPLIT):
            sel[p * LANES + GATE_FA + h, h * HEAD_DIM + p] = 1.0
            sel[p * LANES + GATE_FA + h, D_FOX + h * HEAD_DIM + FOX_SPLIT + p] = -1.0
    return jnp.asarray(sel, BF16)


def _fox_kernel(q_ref, k_ref, v_ref, gt_ref, sel_ref, hn_ref, y_ref, qbias, kbias, u_ref, acc_ref,
                *, seq):
    qi = pl.program_id(1)
    heads = [slice(h * HEAD_DIM, (h + 1) * HEAD_DIM) for h in range(FOX_HEADS)]

    @pl.when(qi == 0)
    def _():
        lane = lax.broadcasted_iota(jnp.int32, (1, 2 * D_FOX), 1)
        in_head = lane % HEAD_DIM
        ones_k = jnp.where(in_head < FOX_SPLIT, 1.0, 0.0)
        ones_q = jnp.where(in_head < 2 * FOX_SPLIT, 1.0, 0.0) - ones_k
        ones = jnp.where(lane >= D_FOX, ones_k, ones_q)
        carry = jnp.zeros((1, LANES), F32)
        for blk in range(seq // FOX_TK):
            rows = slice(blk * FOX_TK, (blk + 1) * FOX_TK)
            cs = _cumsum_rows(_log_sigmoid(gt_ref[rows, :])) + carry
            carry = cs[FOX_TK - 1:FOX_TK, :]
            bias = jnp.dot(_split3(cs * (HEAD_DIM ** 0.5)), sel_ref[...],
                           preferred_element_type=F32) + ones
            qbias[rows, :] = bias[:, :D_FOX].astype(BF16)
            kbias[rows, :] = bias[:, D_FOX:].astype(BF16)

    q_rows = pl.ds(pl.multiple_of(qi * FOX_TQ, FOX_TQ), FOX_TQ)
    q_aug = [jnp.concatenate([q_ref[:, hs], qbias[q_rows, hs]], axis=-1) for hs in heads]
    ones_ext = jnp.ones((FOX_TK, HEAD_DIM), BF16)
    acc_ref[...] = jnp.zeros_like(acc_ref)

    def logits_into(kb, slot):
        ks = pl.ds(pl.multiple_of(kb * FOX_TK, FOX_TK), FOX_TK)
        for h, hs in enumerate(heads):
            k_aug = jnp.concatenate([k_ref[ks, hs], kbias[ks, hs]], axis=-1)
            u_ref[slot, h] = lax.dot_general(q_aug[h], k_aug, NT_DIMS, preferred_element_type=F32)

    def consume(kb, slot, m_old, diagonal):
        ks = pl.ds(pl.multiple_of(kb * FOX_TK, FOX_TK), FOX_TK)
        m_out = []
        for h, hs in enumerate(heads):
            u = u_ref[slot, h]
            if diagonal:
                u = jnp.where(_causal_mask(FOX_TQ), u, -jnp.inf)
            m_new = jnp.maximum(m_old[h], jnp.max(u, axis=-1, keepdims=True))
            p = jnp.exp2((u - m_new) * FOX_EXP2_SCALE)
            alpha = jnp.exp2((m_old[h] - m_new) * FOX_EXP2_SCALE)
            v_ext = jnp.concatenate([v_ref[ks, hs], ones_ext], axis=-1)
            acc_ref[h] = alpha * acc_ref[h] + jnp.dot(p.astype(BF16), v_ext,
                                                      preferred_element_type=F32)
            m_out.append(m_new)
        return tuple(m_out)

    def body(kb, m_old):
        slot = kb % 2
        m_new = consume(kb, slot, m_old, False)
        logits_into(kb + 1, 1 - slot)
        return m_new

    logits_into(0, 0)
    m_init = tuple(jnp.full((FOX_TQ, 1), -jnp.inf, F32) for _ in heads)
    m_run = lax.fori_loop(0, qi, body, m_init)
    consume(qi, qi % 2, m_run, True)

    for h, hs in enumerate(heads):
        acc = acc_ref[h]
        o = acc[:, :HEAD_DIM] / acc[:, HEAD_DIM:]
        y_ref[:, hs] = _rms_rows(o, hn_ref[:, hs]).astype(y_ref.dtype)


def _fox(main, gates, head_norm, batch, seq):
    nq = seq // FOX_TQ
    m = main.shape[0]
    sel = _fox_bias_selectors()
    return pl.pallas_call(
        functools.partial(_fox_kernel, seq=seq),
        grid=(batch, nq),
        in_specs=[
            pl.BlockSpec((FOX_TQ, D_FOX), lambda b, i: (b * nq + i, BLK_QA)),
            pl.BlockSpec((seq, D_FOX), lambda b, i: (b, BLK_KA)),
            pl.BlockSpec((seq, D_FOX), lambda b, i: (b, BLK_VA)),
            pl.BlockSpec((seq, LANES), lambda b, i: (b, 0)),
            pl.BlockSpec(sel.shape, lambda b, i: (0, 0)),
            pl.BlockSpec((1, D_FOX), lambda b, i: (0, 0)),
        ],
        out_specs=pl.BlockSpec((FOX_TQ, D_FOX), lambda b, i: (b * nq + i, 0)),
        out_shape=jax.ShapeDtypeStruct((m, D_FOX), BF16),
        scratch_shapes=[
            pltpu.VMEM((seq, D_FOX), BF16),
            pltpu.VMEM((seq, D_FOX), BF16),
            pltpu.VMEM((2, FOX_HEADS, FOX_TQ, FOX_TK), F32),
            pltpu.VMEM((FOX_HEADS, FOX_TQ, 2 * HEAD_DIM), F32),
        ],
        compiler_params=pltpu.CompilerParams(
            dimension_semantics=("parallel", "arbitrary"),
            vmem_limit_bytes=VMEM_LIMIT),
        name="fox",
    )(main, main, main, gates, sel, head_norm)


CC_TS = 256
CC_HIST = 32
CC_ROWS = 32
CC_SHIFT_ROWS = CC_HIST + CC_TS - SUBLANES


def _cconv_kernel(u_ref, g_ref, w_ref, b_ref, lg_ref, lb_ref, y_ref, ybuf, yshift):
    si = pl.program_id(1)

    @pl.when(si == 0)
    def _():
        ybuf[0:CC_HIST, :] = jnp.zeros((CC_HIST, D_CONV), F32)

    @pl.when(si > 0)
    def _():
        ybuf[0:CC_HIST, :] = ybuf[CC_TS:CC_TS + CC_HIST, :]

    ybuf[CC_HIST:CC_HIST + CC_TS, :] = (
        u_ref[...].astype(F32) * _sigmoid(g_ref[...].astype(F32)))

    first = CC_HIST - (CONV_WIDTH - 1)
    for phase in range(1, SUBLANES):
        yshift[phase - 1] = ybuf[phase:phase + CC_SHIFT_ROWS, :]
    for r in range(CC_TS // CC_ROWS):
        acc = jnp.broadcast_to(b_ref[...], (CC_ROWS, D_CONV))
        for j in range(CONV_WIDTH):
            phase = (first + j) % SUBLANES
            lo = r * CC_ROWS + (first + j) - phase
            if phase == 0:
                tap = ybuf[lo:lo + CC_ROWS, :]
            else:
                tap = yshift[phase - 1, lo:lo + CC_ROWS, :]
            acc = acc + w_ref[j:j + 1, :] * tap
        mu = jnp.mean(acc, axis=-1, keepdims=True)
        xc = acc - mu
        yn = xc * lax.rsqrt(jnp.mean(xc * xc, axis=-1, keepdims=True) + EPS)
        yn = yn * lg_ref[...] + lb_ref[...]
        y_ref[r * CC_ROWS:(r + 1) * CC_ROWS, :] = _swish(yn).astype(y_ref.dtype)


def _cconv(main, w, b, ln_g, ln_b, batch, seq):
    ns = seq // CC_TS
    m = main.shape[0]
    return pl.pallas_call(
        _cconv_kernel,
        grid=(batch, ns),
        in_specs=[
            pl.BlockSpec((CC_TS, D_CONV), lambda bi, s: (bi * ns + s, BLK_UC)),
            pl.BlockSpec((CC_TS, D_CONV), lambda bi, s: (bi * ns + s, BLK_GC)),
            pl.BlockSpec((CONV_WIDTH, D_CONV), lambda bi, s: (0, 0)),
            pl.BlockSpec((1, D_CONV), lambda bi, s: (0, 0)),
            pl.BlockSpec((1, D_CONV), lambda bi, s: (0, 0)),
            pl.BlockSpec((1, D_CONV), lambda bi, s: (0, 0)),
        ],
        out_specs=pl.BlockSpec((CC_TS, D_CONV), lambda bi, s: (bi * ns + s, 0)),
        out_shape=jax.ShapeDtypeStruct((m, D_CONV), BF16),
        scratch_shapes=[
            pltpu.VMEM((CC_HIST + CC_TS, D_CONV), F32),
            pltpu.VMEM((SUBLANES - 1, CC_SHIFT_ROWS, D_CONV), F32),
        ],
        compiler_params=pltpu.CompilerParams(
            dimension_semantics=("parallel", "arbitrary"),
            vmem_limit_bytes=VMEM_LIMIT),
        name="cconv",
    )(main, main, w, b, ln_g, ln_b)


OUT_TM = 1024
OUT_TN = 1024


def _out_proj_kernel(x_ref, ym_ref, ya_ref, yc_ref, wm_ref, wa_ref, wc_ref, o_ref):
    acc = jnp.dot(ym_ref[...], wm_ref[...], preferred_element_type=F32)
    acc = acc + jnp.dot(ya_ref[...], wa_ref[...], preferred_element_type=F32)
    acc = acc + jnp.dot(yc_ref[...], wc_ref[...], preferred_element_type=F32)
    o_ref[...] = x_ref[...] + acc


def _out_proj(x2, ym, ya, yc, w_out, layer):
    m = x2.shape[0]
    fox_blk = D_MLSTM // D_FOX
    return pl.pallas_call(
        _out_proj_kernel,
        grid=(m // OUT_TM, D_MODEL // OUT_TN),
        in_specs=[
            pl.BlockSpec((OUT_TM, OUT_TN), lambda i, j: (i, j)),
            pl.BlockSpec((OUT_TM, D_MLSTM), lambda i, j: (i, 0)),
            pl.BlockSpec((OUT_TM, D_FOX), lambda i, j: (i, 0)),
            pl.BlockSpec((OUT_TM, D_CONV), lambda i, j: (i, 0)),
            pl.BlockSpec((None, D_MLSTM, OUT_TN), lambda i, j: (layer, 0, j)),
            pl.BlockSpec((None, D_FOX, OUT_TN), lambda i, j: (layer, fox_blk, j)),
            pl.BlockSpec((None, D_CONV, OUT_TN), lambda i, j: (layer, fox_blk + 1, j)),
        ],
        out_specs=pl.BlockSpec((OUT_TM, OUT_TN), lambda i, j: (i, j)),
        out_shape=jax.ShapeDtypeStruct((m, D_MODEL), F32),
        compiler_params=pltpu.CompilerParams(
            dimension_semantics=("parallel", "arbitrary"),
            vmem_limit_bytes=VMEM_LIMIT),
        name="out_proj",
    )(x2, ym, ya, yc, w_out, w_out, w_out)


FFN_TM = 1024
FFN_TF = 512
FFN_SLAB = 128


def _ffn_kernel(x_ref, g_ref, wu_ref, wd_ref, fg_ref, o_ref, xn_ref, *, final_norm):
    j = pl.program_id(1)

    @pl.when(j == 0)
    def _():
        def slab(r, carry):
            rows = pl.ds(pl.multiple_of(r * FFN_SLAB, FFN_SLAB), FFN_SLAB)
            x = x_ref[rows, :]
            xn_ref[rows, :] = _rms_rows(x, g_ref[...]).astype(BF16)
            o_ref[rows, :] = x
            return carry
        lax.fori_loop(0, FFN_TM // FFN_SLAB, slab, 0)

    hid = jnp.maximum(jnp.dot(xn_ref[...], wu_ref[...], preferred_element_type=F32), 0.0)
    hid = (hid * hid).astype(BF16)
    o_ref[...] += jnp.dot(hid, wd_ref[...], preferred_element_type=F32)

    if final_norm:
        @pl.when(j == pl.num_programs(1) - 1)
        def _():
            def slab(r, carry):
                rows = pl.ds(pl.multiple_of(r * FFN_SLAB, FFN_SLAB), FFN_SLAB)
                o_ref[rows, :] = _rms_rows(o_ref[rows, :], fg_ref[...])
                return carry
            lax.fori_loop(0, FFN_TM // FFN_SLAB, slab, 0)


def _ffn(x2, g, w_up, w_down, final_g, layer, final_norm):
    m = x2.shape[0]
    return pl.pallas_call(
        functools.partial(_ffn_kernel, final_norm=final_norm),
        grid=(m // FFN_TM, D_FF // FFN_TF),
        in_specs=[
            pl.BlockSpec((FFN_TM, D_MODEL), lambda i, j: (i, 0)),
            pl.BlockSpec((1, D_MODEL), lambda i, j: (0, 0)),
            pl.BlockSpec((None, D_MODEL, FFN_TF), lambda i, j: (layer, 0, j)),
            pl.BlockSpec((None, FFN_TF, D_MODEL), lambda i, j: (layer, j, 0)),
            pl.BlockSpec((1, D_MODEL), lambda i, j: (0, 0)),
        ],
        out_specs=pl.BlockSpec((FFN_TM, D_MODEL), lambda i, j: (i, 0)),
        out_shape=jax.ShapeDtypeStruct((m, D_MODEL), F32),
        scratch_shapes=[pltpu.VMEM((FFN_TM, D_MODEL), BF16)],
        compiler_params=pltpu.CompilerParams(
            dimension_semantics=("parallel", "arbitrary"),
            vmem_limit_bytes=VMEM_LIMIT),
        name="ffn",
    )(x2, g, w_up, w_down, final_g)


OFF_GATE_M = 4 * D_MLSTM
OFF_FOX = OFF_GATE_M + 2 * MLSTM_HEADS
OFF_GATE_A = OFF_FOX + 3 * D_FOX
OFF_CONV = OFF_GATE_A + FOX_HEADS
D_IN = OFF_CONV + 2 * D_CONV
MAIN_FOX = 4 * D_MLSTM
MAIN_CONV = MAIN_FOX + 3 * D_FOX
assert OFF_GATE_M % LANES == 0 and (OFF_GATE_A - GATE_FA) % LANES == 0
PREP_ROWS = 256


def _prep_in_kernel(w_ref, main_ref, gate_ref):
    main_ref[:, 0:MAIN_FOX] = w_ref[:, 0:OFF_GATE_M].astype(BF16)
    main_ref[:, MAIN_FOX:MAIN_CONV] = w_ref[:, OFF_FOX:OFF_GATE_A].astype(BF16)
    main_ref[:, MAIN_CONV:D_MAIN] = w_ref[:, OFF_CONV:D_IN].astype(BF16)
    lane = lax.broadcasted_iota(jnp.int32, (1, LANES), 1)
    gate_m = w_ref[:, OFF_GATE_M:OFF_GATE_M + LANES]
    gate_a = w_ref[:, OFF_GATE_A - GATE_FA:OFF_GATE_A - GATE_FA + LANES]
    gate = jnp.where(lane < GATE_FA, gate_m, jnp.where(lane < GATE_FA + FOX_HEADS, gate_a, 0.0))
    gate_ref[...] = gate.astype(BF16)


def _prep_in_proj(w_in):
    depth, d, _ = w_in.shape
    return pl.pallas_call(
        _prep_in_kernel,
        grid=(depth, d // PREP_ROWS),
        in_specs=[pl.BlockSpec((None, PREP_ROWS, D_IN), lambda l, i: (l, i, 0))],
        out_specs=[
            pl.BlockSpec((None, PREP_ROWS, D_MAIN), lambda l, i: (l, i, 0)),
            pl.BlockSpec((None, PREP_ROWS, LANES), lambda l, i: (l, i, 0)),
        ],
        out_shape=[
            jax.ShapeDtypeStruct((depth, d, D_MAIN), BF16),
            jax.ShapeDtypeStruct((depth, d, LANES), BF16),
        ],
        compiler_params=pltpu.CompilerParams(
            dimension_semantics=("parallel", "parallel"),
            vmem_limit_bytes=VMEM_LIMIT),
        name="prep_in_proj",
    )(w_in)


def _split_bias(b_in):
    main = jnp.concatenate([b_in[..., 0:OFF_GATE_M], b_in[..., OFF_FOX:OFF_GATE_A],
                            b_in[..., OFF_CONV:D_IN]], axis=-1)
    pad = LANES - GATE_FA - FOX_HEADS
    gate = jnp.concatenate([b_in[..., OFF_GATE_M:OFF_FOX], b_in[..., OFF_GATE_A:OFF_CONV],
                            jnp.zeros(b_in.shape[:-1] + (pad,), b_in.dtype)], axis=-1)
    return main, gate


def kernel(x, norm_mix, w_in, b_in, mlstm_conv_w, mlstm_conv_b, mlstm_head_norm, fox_head_norm,
           conv_dw_w, conv_dw_b, conv_ln_g, conv_ln_b, w_out, norm_ffn, w_up, w_down, final_norm):
    batch, seq, d = x.shape
    assert d == D_MODEL and seq % max(FOX_TQ, IN_TM, CC_TS) == 0
    depth = w_in.shape[0]
    x2 = x.reshape(batch * seq, d)
    assert w_in.shape[1:] == (D_MODEL, D_IN)
    w_main, w_gate = _prep_in_proj(w_in)
    b_main, b_gate = _split_bias(b_in)
    w_out_b = w_out.astype(BF16)
    w_up_b = w_up.astype(BF16)
    w_down_b = w_down.astype(BF16)
    row = lambda a: a.reshape(1, -1)

    for l in range(depth):
        main, gates = _in_proj(x2, row(norm_mix[l]), w_main, row(b_main[l]), w_gate, row(b_gate[l]), l)
        ym = _mlstm(main, gates, mlstm_conv_w[l], row(mlstm_conv_b[l]),
                    row(mlstm_head_norm[l]), batch, seq)
        ya = _fox(main, gates, row(fox_head_norm[l]), batch, seq)
        yc = _cconv(main, conv_dw_w[l], row(conv_dw_b[l]), row(conv_ln_g[l]),
                    row(conv_ln_b[l]), batch, seq)
        x2 = _out_proj(x2, ym, ya, yc, w_out_b, l)
        x2 = _ffn(x2, row(norm_ffn[l]), w_up_b, w_down_b, row(final_norm), l,
                  final_norm=(l == depth - 1))
    return x2.reshape(batch, seq, d)
```

```python
import functools
import math

import jax
import jax.numpy as jnp
import numpy as np
from jax import lax
from jax.experimental import pallas as pl
from jax.experimental.pallas import tpu as pltpu

D_MODEL = 2048
DEPTH = 4
HEAD_DIM = 128
D_MLSTM = D_MODEL // 2
D_FOX = D_MODEL // 4
D_CONV = D_MODEL - D_MLSTM - D_FOX
MLSTM_HEADS = D_MLSTM // HEAD_DIM
FOX_HEADS = D_FOX // HEAD_DIM
MLSTM_CONV_WIDTH = 4
CONV_WIDTH = 31
D_FF = 4 * D_MODEL
MLSTM_CHUNK = 128
EPS = 1e-6

LANES = 128
SUBLANES = 8
VMEM_LIMIT = 56 * 1024 * 1024

D_MAIN = 4 * D_MLSTM + 3 * D_FOX + 2 * D_CONV
GATE_I = 0
GATE_F = MLSTM_HEADS
GATE_FA = 2 * MLSTM_HEADS
BLK_QA, BLK_KA, BLK_VA, BLK_UC, BLK_GC = 8, 9, 10, 11, 12

BF16 = jnp.bfloat16
F32 = jnp.float32
NT_DIMS = (((1,), (1,)), ((), ()))
TN_DIMS = (((0,), (0,)), ((), ()))


def _rms_rows(x, g):
    ms = jnp.mean(x * x, axis=-1, keepdims=True)
    return x * lax.rsqrt(ms + EPS) * g


def _sigmoid(x):
    return 0.5 * jnp.tanh(0.5 * x) + 0.5


def _log_sigmoid(x):
    return jnp.minimum(x, 0.0) - jnp.log(1.0 + jnp.exp(-jnp.abs(x)))


def _swish(x, gain=1.0):
    return x * ((0.5 * gain) * jnp.tanh(0.5 * x) + (0.5 * gain))


def _causal_mask(n):
    r = lax.broadcasted_iota(jnp.int32, (n, n), 0)
    c = lax.broadcasted_iota(jnp.int32, (n, n), 1)
    return r >= c


def _split3(x):
    hi = x.astype(BF16)
    rem = x - hi.astype(F32)
    mid = rem.astype(BF16)
    lo = (rem - mid.astype(F32)).astype(BF16)
    return jnp.concatenate([hi, mid, lo], axis=-1)


def _cumsum_rows(x):
    n = x.shape[0]
    tri = jnp.where(_causal_mask(n), 1.0, 0.0).astype(BF16)
    y = jnp.dot(tri, _split3(x), preferred_element_type=F32)
    return y[:, :LANES] + y[:, LANES:2 * LANES] + y[:, 2 * LANES:]


IN_TM = 512
IN_TN = 512
IN_SLAB = 128


def _in_proj_kernel(x_ref, g_ref, w_ref, b_ref, wg_ref, bg_ref, out_ref, gates_ref, xn_ref):
    for r in range(IN_TM // IN_SLAB):
        rows = slice(r * IN_SLAB, (r + 1) * IN_SLAB)
        xn = _rms_rows(x_ref[rows, :], g_ref[...]).astype(BF16)
        xn_ref[rows, :] = xn
        gates_ref[rows, :] = (
            jnp.dot(xn, wg_ref[...], preferred_element_type=F32) + bg_ref[...])

    for n in range(D_MAIN // IN_TN):
        cols = slice(n * IN_TN, (n + 1) * IN_TN)
        acc = jnp.dot(xn_ref[...], w_ref[:, cols], preferred_element_type=F32)
        out_ref[:, cols] = (acc + b_ref[:, cols]).astype(out_ref.dtype)


def _in_proj(x2, g, w_main, b_main, w_gate, b_gate, layer):
    m = x2.shape[0]
    resident = pl.Buffered(1)
    return pl.pallas_call(
        _in_proj_kernel,
        grid=(m // IN_TM,),
        in_specs=[
            pl.BlockSpec((IN_TM, D_MODEL), lambda i: (i, 0)),
            pl.BlockSpec((1, D_MODEL), lambda i: (0, 0), pipeline_mode=resident),
            pl.BlockSpec((None, D_MODEL, D_MAIN), lambda i: (layer, 0, 0), pipeline_mode=resident),
            pl.BlockSpec((1, D_MAIN), lambda i: (0, 0), pipeline_mode=resident),
            pl.BlockSpec((None, D_MODEL, LANES), lambda i: (layer, 0, 0), pipeline_mode=resident),
            pl.BlockSpec((1, LANES), lambda i: (0, 0), pipeline_mode=resident),
        ],
        out_specs=[
            pl.BlockSpec((IN_TM, D_MAIN), lambda i: (i, 0)),
            pl.BlockSpec((IN_TM, LANES), lambda i: (i, 0)),
        ],
        out_shape=[
            jax.ShapeDtypeStruct((m, D_MAIN), BF16),
            jax.ShapeDtypeStruct((m, LANES), F32),
        ],
        scratch_shapes=[pltpu.VMEM((IN_TM, D_MODEL), BF16)],
        compiler_params=pltpu.CompilerParams(
            dimension_semantics=("parallel",),
            vmem_limit_bytes=VMEM_LIMIT),
        name="in_proj",
    )(x2, g, w_main, b_main, w_gate, b_gate)


CONV_PAD = SUBLANES
MLSTM_STEP_CHUNKS = 2


def _gate_prep(gates):
    bcum = _cumsum_rows(_log_sigmoid(gates))
    return gates, bcum, gates.T, bcum.T


def _mlstm_kernel(q_ref, k_ref, v_ref, o_ref, gt_ref, gt_next_ref, wq_ref, wk_ref, bq_ref, bk_ref,
                  hn_ref, y_ref, qbuf, kbuf, c_ref, m_ref, prep_ref):
    L = MLSTM_CHUNK
    step_rows = MLSTM_STEP_CHUNKS * L
    s = pl.program_id(1)
    step = pl.program_id(0) * pl.num_programs(1) + s
    slot = step % 2

    @pl.when(step == 0)
    def _():
        for ci in range(MLSTM_STEP_CHUNKS):
            for idx, tile in enumerate(_gate_prep(gt_ref[ci * L:(ci + 1) * L, :])):
                prep_ref[0, ci, idx] = tile

    @pl.when(s == 0)
    def _():
        qbuf[0:CONV_PAD, :] = jnp.zeros((CONV_PAD, D_MLSTM), F32)
        kbuf[0:CONV_PAD, :] = jnp.zeros((CONV_PAD, D_MLSTM), F32)
        c_ref[...] = jnp.zeros_like(c_ref)
        m_ref[...] = jnp.zeros_like(m_ref)

    @pl.when(s > 0)
    def _():
        qbuf[0:CONV_PAD, :] = qbuf[step_rows:step_rows + CONV_PAD, :]
        kbuf[0:CONV_PAD, :] = kbuf[step_rows:step_rows + CONV_PAD, :]

    qbuf[CONV_PAD:CONV_PAD + step_rows, :] = q_ref[...].astype(F32)
    kbuf[CONV_PAD:CONV_PAD + step_rows, :] = k_ref[...].astype(F32)

    causal = _causal_mask(L)
    ones_ext = jnp.ones((L, HEAD_DIM), BF16)

    def conv_swish(buf, w_ref, b_ref, lo, hs, gain):
        last = MLSTM_CONV_WIDTH - 1
        base = CONV_PAD + lo
        acc = b_ref[:, hs] + w_ref[last:last + 1, hs] * buf[base:base + L, hs]
        for j in range(last):
            off = base - last + j
            acc = acc + w_ref[j:j + 1, hs] * buf[off:off + L, hs]
        return _swish(acc, gain)

    for ci in range(MLSTM_STEP_CHUNKS):
        lo = ci * L
        rows = slice(lo, lo + L)
        gates = prep_ref[slot, ci, 0]
        bcum = prep_ref[slot, ci, 1]
        gates_t = prep_ref[slot, ci, 2]
        bcum_t = prep_ref[slot, ci, 3]

        for h in range(MLSTM_HEADS):
            hs = slice(h * HEAD_DIM, (h + 1) * HEAD_DIM)
            b_c = jnp.broadcast_to(bcum[:, GATE_F + h:GATE_F + h + 1], (L, L))
            i_c = jnp.broadcast_to(gates[:, GATE_I + h:GATE_I + h + 1], (L, L))
            b_r = bcum_t[GATE_F + h:GATE_F + h + 1, :]
            i_r = gates_t[GATE_I + h:GATE_I + h + 1, :]
            m_prev = m_ref[h, 0:1, :]

            d_log = jnp.where(causal, b_c - b_r + i_r, -jnp.inf)
            inter = b_c + m_prev
            m_t = jnp.maximum(inter, jnp.max(d_log, axis=-1, keepdims=True))
            w_intra = jnp.exp(d_log - m_t)
            w_inter = jnp.exp(inter - m_t)

            qh = conv_swish(qbuf, wq_ref, bq_ref, lo, hs, 1.0)
            kh = conv_swish(kbuf, wk_ref, bk_ref, lo, hs, HEAD_DIM ** -0.5)
            qb = qh.astype(BF16)
            kb = kh.astype(BF16)
            v_ext = jnp.concatenate([v_ref[rows, hs], ones_ext], axis=-1)
            c_ext = c_ref[h]

            sc = lax.dot_general(qb, kb, NT_DIMS, preferred_element_type=F32) * w_intra
            qc = jnp.dot(qb, c_ext.astype(BF16), preferred_element_type=F32)
            sv = jnp.dot(sc.astype(BF16), v_ext, preferred_element_type=F32)
            nx = jnp.concatenate([w_inter, w_inter], axis=-1) * qc + sv
            num = nx[:, :HEAD_DIM]
            den = nx[:, HEAD_DIM:]
            hh = num / jnp.maximum(jnp.abs(den), jnp.exp(-m_t))

            b_last = b_c[L - 1:L, :]
            g_c = b_last - b_c + i_c
            m_new = jnp.maximum(b_last + m_prev, jnp.max(g_c, axis=0, keepdims=True))
            w_k = jnp.exp(g_c - m_new)
            decay = jnp.exp(b_last + m_prev - m_new)
            kw = (kh * w_k).astype(BF16)
            upd = lax.dot_general(kw, v_ext, TN_DIMS, preferred_element_type=F32)
            c_ref[h] = jnp.concatenate([decay, decay], axis=-1) * c_ext + upd
            m_ref[h] = jnp.broadcast_to(m_new, (SUBLANES, LANES))

            hn = _rms_rows(hh, hn_ref[:, hs])
            y_ref[rows, hs] = (hn * _sigmoid(o_ref[rows, hs].astype(F32))).astype(y_ref.dtype)

    for ci in range(MLSTM_STEP_CHUNKS):
        for idx, tile in enumerate(_gate_prep(gt_next_ref[ci * L:(ci + 1) * L, :])):
            prep_ref[1 - slot, ci, idx] = tile


def _mlstm(main, gates, conv_w, conv_b, head_norm, batch, seq):
    L = MLSTM_CHUNK
    rows = MLSTM_STEP_CHUNKS * L
    ns = seq // rows
    m = main.shape[0]
    row = lambda b, s: b * ns + s
    return pl.pallas_call(
        _mlstm_kernel,
        grid=(batch, ns),
        in_specs=[
            pl.BlockSpec((rows, D_MLSTM), lambda b, s: (row(b, s), 0)),
            pl.BlockSpec((rows, D_MLSTM), lambda b, s: (row(b, s), 1)),
            pl.BlockSpec((rows, D_MLSTM), lambda b, s: (row(b, s), 2)),
            pl.BlockSpec((rows, D_MLSTM), lambda b, s: (row(b, s), 3)),
            pl.BlockSpec((rows, LANES), lambda b, s: (row(b, s), 0)),
            pl.BlockSpec((rows, LANES), lambda b, s: (jnp.minimum(row(b, s) + 1, m // rows - 1), 0)),
            pl.BlockSpec((MLSTM_CONV_WIDTH, D_MLSTM), lambda b, s: (0, 0)),
            pl.BlockSpec((MLSTM_CONV_WIDTH, D_MLSTM), lambda b, s: (0, 1)),
            pl.BlockSpec((1, D_MLSTM), lambda b, s: (0, 0)),
            pl.BlockSpec((1, D_MLSTM), lambda b, s: (0, 1)),
            pl.BlockSpec((1, D_MLSTM), lambda b, s: (0, 0)),
        ],
        out_specs=pl.BlockSpec((rows, D_MLSTM), lambda b, s: (row(b, s), 0)),
        out_shape=jax.ShapeDtypeStruct((m, D_MLSTM), BF16),
        scratch_shapes=[
            pltpu.VMEM((CONV_PAD + rows, D_MLSTM), F32),
            pltpu.VMEM((CONV_PAD + rows, D_MLSTM), F32),
            pltpu.VMEM((MLSTM_HEADS, HEAD_DIM, 2 * HEAD_DIM), F32),
            pltpu.VMEM((MLSTM_HEADS, SUBLANES, LANES), F32),
            pltpu.VMEM((2, MLSTM_STEP_CHUNKS, 4, L, LANES), F32),
        ],
        compiler_params=pltpu.CompilerParams(
            dimension_semantics=("arbitrary", "arbitrary"),
            vmem_limit_bytes=VMEM_LIMIT),
        name="mlstm",
    )(main, main, main, main, gates, gates, conv_w, conv_w, conv_b, conv_b, head_norm)


FOX_TQ = 256
FOX_TK = 256
FOX_SPLIT = 3
FOX_EXP2_SCALE = (HEAD_DIM ** -0.5) * math.log2(math.e)


def _fox_bias_selectors():
    sel = np.zeros((FOX_SPLIT * LANES, 2 * D_FOX), np.float32)
    for h in range(FOX_HEADS):
        for p in range(FOX_SPLIT):
            sel[p * LANES + GATE_FA + h, h * HEAD_DIM + p] = 1.0
            sel[p * LANES + GATE_FA + h, D_FOX + h * HEAD_DIM + FOX_SPLIT + p] = -1.0
    return jnp.asarray(sel, BF16)


def _fox_kernel(q_ref, k_ref, v_ref, gt_ref, sel_ref, hn_ref, y_ref, qbias, kbias, u_ref, acc_ref,
                *, seq):
    qi = pl.program_id(1)
    heads = [slice(h * HEAD_DIM, (h + 1) * HEAD_DIM) for h in range(FOX_HEADS)]

    @pl.when(qi == 0)
    def _():
        lane = lax.broadcasted_iota(jnp.int32, (1, 2 * D_FOX), 1)
        in_head = lane % HEAD_DIM
        ones_k = jnp.where(in_head < FOX_SPLIT, 1.0, 0.0)
        ones_q = jnp.where(in_head < 2 * FOX_SPLIT, 1.0, 0.0) - ones_k
        ones = jnp.where(lane >= D_FOX, ones_k, ones_q)
        carry = jnp.zeros((1, LANES), F32)
        for blk in range(seq // FOX_TK):
            rows = slice(blk * FOX_TK, (blk + 1) * FOX_TK)
            cs = _cumsum_rows(_log_sigmoid(gt_ref[rows, :])) + carry
            carry = cs[FOX_TK - 1:FOX_TK, :]
            bias = jnp.dot(_split3(cs * (HEAD_DIM ** 0.5)), sel_ref[...],
                           preferred_element_type=F32) + ones
            qbias[rows, :] = bias[:, :D_FOX].astype(BF16)
            kbias[rows, :] = bias[:, D_FOX:].astype(BF16)

    q_rows = pl.ds(pl.multiple_of(qi * FOX_TQ, FOX_TQ), FOX_TQ)
    q_aug = [jnp.concatenate([q_ref[:, hs], qbias[q_rows, hs]], axis=-1) for hs in heads]
    ones_ext = jnp.ones((FOX_TK, HEAD_DIM), BF16)
    acc_ref[...] = jnp.zeros_like(acc_ref)

    def logits_into(kb, slot):
        ks = pl.ds(pl.multiple_of(kb * FOX_TK, FOX_TK), FOX_TK)
        for h, hs in enumerate(heads):
            k_aug = jnp.concatenate([k_ref[ks, hs], kbias[ks, hs]], axis=-1)
            u_ref[slot, h] = lax.dot_general(q_aug[h], k_aug, NT_DIMS, preferred_element_type=F32)

    def consume(kb, slot, m_old, diagonal):
        ks = pl.ds(pl.multiple_of(kb * FOX_TK, FOX_TK), FOX_TK)
        m_out = []
        for h, hs in enumerate(heads):
            u = u_ref[slot, h]
            if diagonal:
                u = jnp.where(_causal_mask(FOX_TQ), u, -jnp.inf)
            m_new = jnp.maximum(m_old[h], jnp.max(u, axis=-1, keepdims=True))
            p = jnp.exp2((u - m_new) * FOX_EXP2_SCALE)
            alpha = jnp.exp2((m_old[h] - m_new) * FOX_EXP2_SCALE)
            v_ext = jnp.concatenate([v_ref[ks, hs], ones_ext], axis=-1)
            acc_ref[h] = alpha * acc_ref[h] + jnp.dot(p.astype(BF16), v_ext,
                                                      preferred_element_type=F32)
            m_out.append(m_new)
        return tuple(m_out)

    def body(kb, m_old):
        slot = kb % 2
        m_new = consume(kb, slot, m_old, False)
        logits_into(kb + 1, 1 - slot)
        return m_new

    logits_into(0, 0)
    m_init = tuple(jnp.full((FOX_TQ, 1), -jnp.inf, F32) for _ in heads)
    m_run = lax.fori_loop(0, qi, body, m_init)
    consume(qi, qi % 2, m_run, True)

    for h, hs in enumerate(heads):
        acc = acc_ref[h]
        o = acc[:, :HEAD_DIM] / acc[:, HEAD_DIM:]
        y_ref[:, hs] = _rms_rows(o, hn_ref[:, hs]).astype(y_ref.dtype)


def _fox(main, gates, head_norm, batch, seq):
    nq = seq // FOX_TQ
    m = main.shape[0]
    sel = _fox_bias_selectors()
    return pl.pallas_call(
        functools.partial(_fox_kernel, seq=seq),
        grid=(batch, nq),
        in_specs=[
            pl.BlockSpec((FOX_TQ, D_FOX), lambda b, i: (b * nq + i, BLK_QA)),
            pl.BlockSpec((seq, D_FOX), lambda b, i: (b, BLK_KA)),
            pl.BlockSpec((seq, D_FOX), lambda b, i: (b, BLK_VA)),
            pl.BlockSpec((seq, LANES), lambda b, i: (b, 0)),
            pl.BlockSpec(sel.shape, lambda b, i: (0, 0)),
            pl.BlockSpec((1, D_FOX), lambda b, i: (0, 0)),
        ],
        out_specs=pl.BlockSpec((FOX_TQ, D_FOX), lambda b, i: (b * nq + i, 0)),
        out_shape=jax.ShapeDtypeStruct((m, D_FOX), BF16),
        scratch_shapes=[
            pltpu.VMEM((seq, D_FOX), BF16),
            pltpu.VMEM((seq, D_FOX), BF16),
            pltpu.VMEM((2, FOX_HEADS, FOX_TQ, FOX_TK), F32),
            pltpu.VMEM((FOX_HEADS, FOX_TQ, 2 * HEAD_DIM), F32),
        ],
        compiler_params=pltpu.CompilerParams(
            dimension_semantics=("parallel", "arbitrary"),
            vmem_limit_bytes=VMEM_LIMIT),
        name="fox",
    )(main, main, main, gates, sel, head_norm)


CC_TS = 512
CC_HIST = 32
CC_ROWS = 32
CC_SHIFT_ROWS = CC_HIST + CC_TS - SUBLANES


def _cconv_kernel(u_ref, g_ref, w_ref, b_ref, lg_ref, lb_ref, y_ref, ybuf, yshift):
    si = pl.program_id(1)

    @pl.when(si == 0)
    def _():
        ybuf[0:CC_HIST, :] = jnp.zeros((CC_HIST, D_CONV), F32)

    @pl.when(si > 0)
    def _():
        ybuf[0:CC_HIST, :] = ybuf[CC_TS:CC_TS + CC_HIST, :]

    ybuf[CC_HIST:CC_HIST + CC_TS, :] = (
        u_ref[...].astype(F32) * _sigmoid(g_ref[...].astype(F32)))

    first = CC_HIST - (CONV_WIDTH - 1)
    for phase in range(1, SUBLANES):
        yshift[phase - 1] = ybuf[phase:phase + CC_SHIFT_ROWS, :]
    for r in range(CC_TS // CC_ROWS):
        acc = jnp.broadcast_to(b_ref[...], (CC_ROWS, D_CONV))
        for j in range(CONV_WIDTH):
            phase = (first + j) % SUBLANES
            lo = r * CC_ROWS + (first + j) - phase
            if phase == 0:
                tap = ybuf[lo:lo + CC_ROWS, :]
            else:
                tap = yshift[phase - 1, lo:lo + CC_ROWS, :]
            acc = acc + w_ref[j:j + 1, :] * tap
        mu = jnp.mean(acc, axis=-1, keepdims=True)
        xc = acc - mu
        yn = xc * lax.rsqrt(jnp.mean(xc * xc, axis=-1, keepdims=True) + EPS)
        yn = yn * lg_ref[...] + lb_ref[...]
        y_ref[r * CC_ROWS:(r + 1) * CC_ROWS, :] = _swish(yn).astype(y_ref.dtype)


def _cconv(main, w, b, ln_g, ln_b, batch, seq):
    ns = seq // CC_TS
    m = main.shape[0]
    return pl.pallas_call(
        _cconv_kernel,
        grid=(batch, ns),
        in_specs=[
            pl.BlockSpec((CC_TS, D_CONV), lambda bi, s: (bi * ns + s, BLK_UC)),
            pl.BlockSpec((CC_TS, D_CONV), lambda bi, s: (bi * ns + s, BLK_GC)),
            pl.BlockSpec((CONV_WIDTH, D_CONV), lambda bi, s: (0, 0)),
            pl.BlockSpec((1, D_CONV), lambda bi, s: (0, 0)),
            pl.BlockSpec((1, D_CONV), lambda bi, s: (0, 0)),
            pl.BlockSpec((1, D_CONV), lambda bi, s: (0, 0)),
        ],
        out_specs=pl.BlockSpec((CC_TS, D_CONV), lambda bi, s: (bi * ns + s, 0)),
        out_shape=jax.ShapeDtypeStruct((m, D_CONV), BF16),
        scratch_shapes=[
            pltpu.VMEM((CC_HIST + CC_TS, D_CONV), F32),
            pltpu.VMEM((SUBLANES - 1, CC_SHIFT_ROWS, D_CONV), F32),
        ],
        compiler_params=pltpu.CompilerParams(
            dimension_semantics=("parallel", "arbitrary"),
            vmem_limit_bytes=VMEM_LIMIT),
        name="cconv",
    )(main, main, w, b, ln_g, ln_b)


OUT_TM = 512
OUT_TN = 512


def _out_proj_kernel(x_ref, ym_ref, ya_ref, yc_ref, w_ref, o_ref):
    for n in range(D_MODEL // OUT_TN):
        cols = slice(n * OUT_TN, (n + 1) * OUT_TN)
        acc = jnp.dot(ym_ref[...], w_ref[0:D_MLSTM, cols], preferred_element_type=F32)
        acc = acc + jnp.dot(ya_ref[...], w_ref[D_MLSTM:D_MLSTM + D_FOX, cols],
                            preferred_element_type=F32)
        acc = acc + jnp.dot(yc_ref[...], w_ref[D_MLSTM + D_FOX:D_MODEL, cols],
                            preferred_element_type=F32)
        o_ref[:, cols] = x_ref[:, cols] + acc


def _out_proj(x2, ym, ya, yc, w_out, layer):
    m = x2.shape[0]
    return pl.pallas_call(
        _out_proj_kernel,
        grid=(m // OUT_TM,),
        in_specs=[
            pl.BlockSpec((OUT_TM, D_MODEL), lambda i: (i, 0)),
            pl.BlockSpec((OUT_TM, D_MLSTM), lambda i: (i, 0)),
            pl.BlockSpec((OUT_TM, D_FOX), lambda i: (i, 0)),
            pl.BlockSpec((OUT_TM, D_CONV), lambda i: (i, 0)),
            pl.BlockSpec((None, D_MODEL, D_MODEL), lambda i: (layer, 0, 0),
                         pipeline_mode=pl.Buffered(1)),
        ],
        out_specs=pl.BlockSpec((OUT_TM, D_MODEL), lambda i: (i, 0)),
        out_shape=jax.ShapeDtypeStruct((m, D_MODEL), F32),
        compiler_params=pltpu.CompilerParams(
            dimension_semantics=("parallel",),
            vmem_limit_bytes=VMEM_LIMIT),
        name="out_proj",
    )(x2, ym, ya, yc, w_out)


FFN_TM = 1024
FFN_TF = 512
FFN_SLAB = 128


def _ffn_kernel(x_ref, g_ref, wu_ref, wd_ref, fg_ref, o_ref, xn_ref, *, final_norm):
    j = pl.program_id(1)

    @pl.when(j == 0)
    def _():
        for r in range(FFN_TM // FFN_SLAB):
            rows = slice(r * FFN_SLAB, (r + 1) * FFN_SLAB)
            x = x_ref[rows, :]
            xn_ref[rows, :] = _rms_rows(x, g_ref[...]).astype(BF16)
            o_ref[rows, :] = x

    hid = jnp.maximum(jnp.dot(xn_ref[...], wu_ref[...], preferred_element_type=F32), 0.0)
    hid = (hid * hid).astype(BF16)
    o_ref[...] += jnp.dot(hid, wd_ref[...], preferred_element_type=F32)

    if final_norm:
        @pl.when(j == pl.num_programs(1) - 1)
        def _():
            for r in range(FFN_TM // FFN_SLAB):
                rows = slice(r * FFN_SLAB, (r + 1) * FFN_SLAB)
                o_ref[rows, :] = _rms_rows(o_ref[rows, :], fg_ref[...])


def _ffn(x2, g, w_up, w_down, final_g, layer, final_norm):
    m = x2.shape[0]
    return pl.pallas_call(
        functools.partial(_ffn_kernel, final_norm=final_norm),
        grid=(m // FFN_TM, D_FF // FFN_TF),
        in_specs=[
            pl.BlockSpec((FFN_TM, D_MODEL), lambda i, j: (i, 0)),
            pl.BlockSpec((1, D_MODEL), lambda i, j: (0, 0)),
            pl.BlockSpec((None, D_MODEL, FFN_TF), lambda i, j: (layer, 0, j)),
            pl.BlockSpec((None, FFN_TF, D_MODEL), lambda i, j: (layer, j, 0)),
            pl.BlockSpec((1, D_MODEL), lambda i, j: (0, 0)),
        ],
        out_specs=pl.BlockSpec((FFN_TM, D_MODEL), lambda i, j: (i, 0)),
        out_shape=jax.ShapeDtypeStruct((m, D_MODEL), F32),
        scratch_shapes=[pltpu.VMEM((FFN_TM, D_MODEL), BF16)],
        compiler_params=pltpu.CompilerParams(
            dimension_semantics=("parallel", "arbitrary"),
            vmem_limit_bytes=VMEM_LIMIT),
        name="ffn",
    )(x2, g, w_up, w_down, final_g)


OFF_GATE_M = 4 * D_MLSTM
OFF_FOX = OFF_GATE_M + 2 * MLSTM_HEADS
OFF_GATE_A = OFF_FOX + 3 * D_FOX
OFF_CONV = OFF_GATE_A + FOX_HEADS
D_IN = OFF_CONV + 2 * D_CONV
MAIN_FOX = 4 * D_MLSTM
MAIN_CONV = MAIN_FOX + 3 * D_FOX
assert OFF_GATE_M % LANES == 0 and (OFF_GATE_A - GATE_FA) % LANES == 0
PREP_ROWS = 256


def _prep_in_kernel(w_ref, main_ref, gate_ref):
    main_ref[:, 0:MAIN_FOX] = w_ref[:, 0:OFF_GATE_M].astype(BF16)
    main_ref[:, MAIN_FOX:MAIN_CONV] = w_ref[:, OFF_FOX:OFF_GATE_A].astype(BF16)
    main_ref[:, MAIN_CONV:D_MAIN] = w_ref[:, OFF_CONV:D_IN].astype(BF16)
    lane = lax.broadcasted_iota(jnp.int32, (1, LANES), 1)
    gate_m = w_ref[:, OFF_GATE_M:OFF_GATE_M + LANES]
    gate_a = w_ref[:, OFF_GATE_A - GATE_FA:OFF_GATE_A - GATE_FA + LANES]
    gate = jnp.where(lane < GATE_FA, gate_m, jnp.where(lane < GATE_FA + FOX_HEADS, gate_a, 0.0))
    gate_ref[...] = gate.astype(BF16)


def _prep_in_proj(w_in):
    depth, d, _ = w_in.shape
    return pl.pallas_call(
        _prep_in_kernel,
        grid=(depth, d // PREP_ROWS),
        in_specs=[pl.BlockSpec((None, PREP_ROWS, D_IN), lambda l, i: (l, i, 0))],
        out_specs=[
            pl.BlockSpec((None, PREP_ROWS, D_MAIN), lambda l, i: (l, i, 0)),
            pl.BlockSpec((None, PREP_ROWS, LANES), lambda l, i: (l, i, 0)),
        ],
        out_shape=[
            jax.ShapeDtypeStruct((depth, d, D_MAIN), BF16),
            jax.ShapeDtypeStruct((depth, d, LANES), BF16),
        ],
        compiler_params=pltpu.CompilerParams(
            dimension_semantics=("parallel", "parallel"),
            vmem_limit_bytes=VMEM_LIMIT),
        name="prep_in_proj",
    )(w_in)


def _split_bias(b_in):
    main = jnp.concatenate([b_in[..., 0:OFF_GATE_M], b_in[..., OFF_FOX:OFF_GATE_A],
                            b_in[..., OFF_CONV:D_IN]], axis=-1)
    pad = LANES - GATE_FA - FOX_HEADS
    gate = jnp.concatenate([b_in[..., OFF_GATE_M:OFF_FOX], b_in[..., OFF_GATE_A:OFF_CONV],
                            jnp.zeros(b_in.shape[:-1] + (pad,), b_in.dtype)], axis=-1)
    return main, gate


def kernel(x, norm_mix, w_in, b_in, mlstm_conv_w, mlstm_conv_b, mlstm_head_norm, fox_head_norm,
           conv_dw_w, conv_dw_b, conv_ln_g, conv_ln_b, w_out, norm_ffn, w_up, w_down, final_norm):
    batch, seq, d = x.shape
    assert d == D_MODEL and seq % max(FOX_TQ, IN_TM, CC_TS, MLSTM_STEP_CHUNKS * MLSTM_CHUNK) == 0
    depth = w_in.shape[0]
    x2 = x.reshape(batch * seq, d)
    assert w_in.shape[1:] == (D_MODEL, D_IN)
    w_main, w_gate = _prep_in_proj(w_in)
    b_main, b_gate = _split_bias(b_in)
    w_out_b = w_out.astype(BF16)
    w_up_b = w_up.astype(BF16)
    w_down_b = w_down.astype(BF16)
    row = lambda a: a.reshape(1, -1)

    for l in range(depth):
        main, gates = _in_proj(x2, row(norm_mix[l]), w_main, row(b_main[l]), w_gate, row(b_gate[l]), l)
        ym = _mlstm(main, gates, mlstm_conv_w[l], row(mlstm_conv_b[l]),
                    row(mlstm_head_norm[l]), batch, seq)
        ya = _fox(main, gates, row(fox_head_norm[l]), batch, seq)
        yc = _cconv(main, conv_dw_w[l], row(conv_dw_b[l]), row(conv_ln_g[l]),
                    row(conv_ln_b[l]), batch, seq)
        x2 = _out_proj(x2, ym, ya, yc, w_out_b, l)
        x2 = _ffn(x2, row(norm_ffn[l]), w_up_b, w_down_b, row(final_norm), l,
                  final_norm=(l == depth - 1))
    return x2.reshape(batch, seq, d)
```

```python
import functools
import math

import jax
import jax.numpy as jnp
import numpy as np
from jax import lax
from jax.experimental import pallas as pl
from jax.experimental.pallas import tpu as pltpu

D_MODEL = 2048
DEPTH = 4
HEAD_DIM = 128
D_MLSTM = D_MODEL // 2
D_FOX = D_MODEL // 4
D_CONV = D_MODEL - D_MLSTM - D_FOX
MLSTM_HEADS = D_MLSTM // HEAD_DIM
FOX_HEADS = D_FOX // HEAD_DIM
MLSTM_CONV_WIDTH = 4
CONV_WIDTH = 31
D_FF = 4 * D_MODEL
MLSTM_CHUNK = 128
EPS = 1e-6

LANES = 128
SUBLANES = 8
VMEM_LIMIT = 56 * 1024 * 1024

D_MAIN = 4 * D_MLSTM + 3 * D_FOX + 2 * D_CONV
GATE_I = 0
GATE_F = MLSTM_HEADS
GATE_FA = 2 * MLSTM_HEADS
BLK_QA, BLK_KA, BLK_VA, BLK_UC, BLK_GC = 8, 9, 10, 11, 12

BF16 = jnp.bfloat16
F32 = jnp.float32
NT_DIMS = (((1,), (1,)), ((), ()))
TN_DIMS = (((0,), (0,)), ((), ()))


def _rms_rows(x, g):
    ms = jnp.mean(x * x, axis=-1, keepdims=True)
    return x * lax.rsqrt(ms + EPS) * g


def _sigmoid(x):
    return 0.5 * jnp.tanh(0.5 * x) + 0.5


def _log_sigmoid(x):
    return jnp.minimum(x, 0.0) - jnp.log(1.0 + jnp.exp(-jnp.abs(x)))


def _swish(x, gain=1.0):
    return x * ((0.5 * gain) * jnp.tanh(0.5 * x) + (0.5 * gain))


def _causal_mask(n):
    r = lax.broadcasted_iota(jnp.int32, (n, n), 0)
    c = lax.broadcasted_iota(jnp.int32, (n, n), 1)
    return r >= c


def _split3(x):
    hi = x.astype(BF16)
    rem = x - hi.astype(F32)
    mid = rem.astype(BF16)
    lo = (rem - mid.astype(F32)).astype(BF16)
    return jnp.concatenate([hi, mid, lo], axis=-1)


def _cumsum_rows(x):
    n = x.shape[0]
    tri = jnp.where(_causal_mask(n), 1.0, 0.0).astype(BF16)
    y = jnp.dot(tri, _split3(x), preferred_element_type=F32)
    return y[:, :LANES] + y[:, LANES:2 * LANES] + y[:, 2 * LANES:]


IN_TM = 512
IN_TN = 512
IN_SLAB = 128


def _in_proj_kernel(x_ref, g_ref, w_ref, b_ref, wg_ref, bg_ref, out_ref, gates_ref, xn_ref):
    for r in range(IN_TM // IN_SLAB):
        rows = slice(r * IN_SLAB, (r + 1) * IN_SLAB)
        xn_ref[rows, :] = _rms_rows(x_ref[rows, :], g_ref[...]).astype(BF16)

    gates_ref[...] = jnp.dot(xn_ref[...], wg_ref[...], preferred_element_type=F32) + bg_ref[...]
    for n in range(D_MAIN // IN_TN):
        cols = slice(n * IN_TN, (n + 1) * IN_TN)
        acc = jnp.dot(xn_ref[...], w_ref[:, cols], preferred_element_type=F32)
        out_ref[:, cols] = (acc + b_ref[:, cols]).astype(out_ref.dtype)


def _in_proj(x2, g, w_main, b_main, w_gate, b_gate, layer):
    m = x2.shape[0]
    resident = pl.Buffered(1)
    return pl.pallas_call(
        _in_proj_kernel,
        grid=(m // IN_TM,),
        in_specs=[
            pl.BlockSpec((IN_TM, D_MODEL), lambda i: (i, 0)),
            pl.BlockSpec((1, D_MODEL), lambda i: (0, 0), pipeline_mode=resident),
            pl.BlockSpec((None, D_MODEL, D_MAIN), lambda i: (layer, 0, 0), pipeline_mode=resident),
            pl.BlockSpec((1, D_MAIN), lambda i: (0, 0), pipeline_mode=resident),
            pl.BlockSpec((None, D_MODEL, LANES), lambda i: (layer, 0, 0), pipeline_mode=resident),
            pl.BlockSpec((1, LANES), lambda i: (0, 0), pipeline_mode=resident),
        ],
        out_specs=[
            pl.BlockSpec((IN_TM, D_MAIN), lambda i: (i, 0)),
            pl.BlockSpec((IN_TM, LANES), lambda i: (i, 0)),
        ],
        out_shape=[
            jax.ShapeDtypeStruct((m, D_MAIN), BF16),
            jax.ShapeDtypeStruct((m, LANES), F32),
        ],
        scratch_shapes=[pltpu.VMEM((IN_TM, D_MODEL), BF16)],
        compiler_params=pltpu.CompilerParams(
            dimension_semantics=("parallel",),
            vmem_limit_bytes=VMEM_LIMIT),
        name="in_proj",
    )(x2, g, w_main, b_main, w_gate, b_gate)


CONV_PAD = SUBLANES
MLSTM_STEP_CHUNKS = 2


def _gate_prep(gates):
    bcum = _cumsum_rows(_log_sigmoid(gates))
    return gates, bcum, gates.T, bcum.T


def _mlstm_kernel(q_ref, k_ref, v_ref, o_ref, gt_ref, gt_next_ref, wq_ref, wk_ref, bq_ref, bk_ref,
                  hn_ref, y_ref, qbuf, kbuf, c_ref, m_ref, prep_ref):
    L = MLSTM_CHUNK
    step_rows = MLSTM_STEP_CHUNKS * L
    s = pl.program_id(1)
    step = pl.program_id(0) * pl.num_programs(1) + s
    slot = step % 2

    @pl.when(step == 0)
    def _():
        for ci in range(MLSTM_STEP_CHUNKS):
            for idx, tile in enumerate(_gate_prep(gt_ref[ci * L:(ci + 1) * L, :])):
                prep_ref[0, ci, idx] = tile

    @pl.when(s == 0)
    def _():
        qbuf[0:CONV_PAD, :] = jnp.zeros((CONV_PAD, D_MLSTM), F32)
        kbuf[0:CONV_PAD, :] = jnp.zeros((CONV_PAD, D_MLSTM), F32)
        c_ref[...] = jnp.zeros_like(c_ref)
        m_ref[...] = jnp.zeros_like(m_ref)

    @pl.when(s > 0)
    def _():
        qbuf[0:CONV_PAD, :] = qbuf[step_rows:step_rows + CONV_PAD, :]
        kbuf[0:CONV_PAD, :] = kbuf[step_rows:step_rows + CONV_PAD, :]

    qbuf[CONV_PAD:CONV_PAD + step_rows, :] = q_ref[...].astype(F32)
    kbuf[CONV_PAD:CONV_PAD + step_rows, :] = k_ref[...].astype(F32)

    causal = _causal_mask(L)
    ones_ext = jnp.ones((L, HEAD_DIM), BF16)

    def conv_swish(buf, w_ref, b_ref, lo, hs, gain):
        last = MLSTM_CONV_WIDTH - 1
        base = CONV_PAD + lo
        acc = b_ref[:, hs] + w_ref[last:last + 1, hs] * buf[base:base + L, hs]
        for j in range(last):
            off = base - last + j
            acc = acc + w_ref[j:j + 1, hs] * buf[off:off + L, hs]
        return _swish(acc, gain)

    for ci in range(MLSTM_STEP_CHUNKS):
        lo = ci * L
        rows = slice(lo, lo + L)
        gates = prep_ref[slot, ci, 0]
        bcum = prep_ref[slot, ci, 1]
        gates_t = prep_ref[slot, ci, 2]
        bcum_t = prep_ref[slot, ci, 3]

        for h in range(MLSTM_HEADS):
            hs = slice(h * HEAD_DIM, (h + 1) * HEAD_DIM)
            b_c = jnp.broadcast_to(bcum[:, GATE_F + h:GATE_F + h + 1], (L, L))
            i_c = jnp.broadcast_to(gates[:, GATE_I + h:GATE_I + h + 1], (L, L))
            b_r = bcum_t[GATE_F + h:GATE_F + h + 1, :]
            i_r = gates_t[GATE_I + h:GATE_I + h + 1, :]
            m_prev = m_ref[h, 0:1, :]

            d_log = jnp.where(causal, b_c - b_r + i_r, -jnp.inf)
            inter = b_c + m_prev
            m_t = jnp.maximum(inter, jnp.max(d_log, axis=-1, keepdims=True))
            w_intra = jnp.exp(d_log - m_t)
            w_inter = jnp.exp(inter - m_t)

            qh = conv_swish(qbuf, wq_ref, bq_ref, lo, hs, 1.0)
            kh = conv_swish(kbuf, wk_ref, bk_ref, lo, hs, HEAD_DIM ** -0.5)
            qb = qh.astype(BF16)
            kb = kh.astype(BF16)
            v_ext = jnp.concatenate([v_ref[rows, hs], ones_ext], axis=-1)
            c_ext = c_ref[h]

            sc = lax.dot_general(qb, kb, NT_DIMS, preferred_element_type=F32) * w_intra
            qc = jnp.dot(qb, c_ext.astype(BF16), preferred_element_type=F32)
            sv = jnp.dot(sc.astype(BF16), v_ext, preferred_element_type=F32)
            nx = jnp.concatenate([w_inter, w_inter], axis=-1) * qc + sv
            num = nx[:, :HEAD_DIM]
            den = nx[:, HEAD_DIM:]
            hh = num / jnp.maximum(jnp.abs(den), jnp.exp(-m_t))

            b_last = b_c[L - 1:L, :]
            g_c = b_last - b_c + i_c
            m_new = jnp.maximum(b_last + m_prev, jnp.max(g_c, axis=0, keepdims=True))
            w_k = jnp.exp(g_c - m_new)
            decay = jnp.exp(b_last + m_prev - m_new)
            kw = (kh * w_k).astype(BF16)
            upd = lax.dot_general(kw, v_ext, TN_DIMS, preferred_element_type=F32)
            c_ref[h] = jnp.concatenate([decay, decay], axis=-1) * c_ext + upd
            m_ref[h] = jnp.broadcast_to(m_new, (SUBLANES, LANES))

            hn = _rms_rows(hh, hn_ref[:, hs])
            y_ref[rows, hs] = (hn * _sigmoid(o_ref[rows, hs].astype(F32))).astype(y_ref.dtype)

    for ci in range(MLSTM_STEP_CHUNKS):
        for idx, tile in enumerate(_gate_prep(gt_next_ref[ci * L:(ci + 1) * L, :])):
            prep_ref[1 - slot, ci, idx] = tile


def _mlstm(main, gates, conv_w, conv_b, head_norm, batch, seq):
    L = MLSTM_CHUNK
    rows = MLSTM_STEP_CHUNKS * L
    ns = seq // rows
    m = main.shape[0]
    row = lambda b, s: b * ns + s
    return pl.pallas_call(
        _mlstm_kernel,
        grid=(batch, ns),
        in_specs=[
            pl.BlockSpec((rows, D_MLSTM), lambda b, s: (row(b, s), 0)),
            pl.BlockSpec((rows, D_MLSTM), lambda b, s: (row(b, s), 1)),
            pl.BlockSpec((rows, D_MLSTM), lambda b, s: (row(b, s), 2)),
            pl.BlockSpec((rows, D_MLSTM), lambda b, s: (row(b, s), 3)),
            pl.BlockSpec((rows, LANES), lambda b, s: (row(b, s), 0)),
            pl.BlockSpec((rows, LANES), lambda b, s: (jnp.minimum(row(b, s) + 1, m // rows - 1), 0)),
            pl.BlockSpec((MLSTM_CONV_WIDTH, D_MLSTM), lambda b, s: (0, 0)),
            pl.BlockSpec((MLSTM_CONV_WIDTH, D_MLSTM), lambda b, s: (0, 1)),
            pl.BlockSpec((1, D_MLSTM), lambda b, s: (0, 0)),
            pl.BlockSpec((1, D_MLSTM), lambda b, s: (0, 1)),
            pl.BlockSpec((1, D_MLSTM), lambda b, s: (0, 0)),
        ],
        out_specs=pl.BlockSpec((rows, D_MLSTM), lambda b, s: (row(b, s), 0)),
        out_shape=jax.ShapeDtypeStruct((m, D_MLSTM), BF16),
        scratch_shapes=[
            pltpu.VMEM((CONV_PAD + rows, D_MLSTM), F32),
            pltpu.VMEM((CONV_PAD + rows, D_MLSTM), F32),
            pltpu.VMEM((MLSTM_HEADS, HEAD_DIM, 2 * HEAD_DIM), F32),
            pltpu.VMEM((MLSTM_HEADS, SUBLANES, LANES), F32),
            pltpu.VMEM((2, MLSTM_STEP_CHUNKS, 4, L, LANES), F32),
        ],
        compiler_params=pltpu.CompilerParams(
            dimension_semantics=("arbitrary", "arbitrary"),
            vmem_limit_bytes=VMEM_LIMIT),
        name="mlstm",
    )(main, main, main, main, gates, gates, conv_w, conv_w, conv_b, conv_b, head_norm)


FOX_TQ = 256
FOX_TK = 256
FOX_SPLIT = 3
FOX_EXP2_SCALE = (HEAD_DIM ** -0.5) * math.log2(math.e)


def _fox_bias_selectors():
    sel = np.zeros((FOX_SPLIT * LANES, 2 * D_FOX), np.float32)
    for h in range(FOX_HEADS):
        for p in range(FOX_SPLIT):
            sel[p * LANES + GATE_FA + h, h * HEAD_DIM + p] = 1.0
            sel[p * LANES + GATE_FA + h, D_FOX + h * HEAD_DIM + FOX_SPLIT + p] = -1.0
    return jnp.asarray(sel, BF16)


def _fox_kernel(q_ref, k_ref, v_ref, gt_ref, sel_ref, hn_ref, y_ref, qbias, kbias, u_ref, m_ref,
                acc_ref, *, seq):
    qi = pl.program_id(1)
    heads = [slice(h * HEAD_DIM, (h + 1) * HEAD_DIM) for h in range(FOX_HEADS)]

    @pl.when(qi == 0)
    def _():
        lane = lax.broadcasted_iota(jnp.int32, (1, 2 * D_FOX), 1)
        in_head = lane % HEAD_DIM
        ones_k = jnp.where(in_head < FOX_SPLIT, 1.0, 0.0)
        ones_q = jnp.where(in_head < 2 * FOX_SPLIT, 1.0, 0.0) - ones_k
        ones = jnp.where(lane >= D_FOX, ones_k, ones_q)
        carry = jnp.zeros((1, LANES), F32)
        for blk in range(seq // FOX_TK):
            rows = slice(blk * FOX_TK, (blk + 1) * FOX_TK)
            cs = _cumsum_rows(_log_sigmoid(gt_ref[rows, :])) + carry
            carry = cs[FOX_TK - 1:FOX_TK, :]
            bias = jnp.dot(_split3(cs * (HEAD_DIM ** 0.5)), sel_ref[...],
                           preferred_element_type=F32) + ones
            qbias[rows, :] = bias[:, :D_FOX].astype(BF16)
            kbias[rows, :] = bias[:, D_FOX:].astype(BF16)

    q_rows = pl.ds(pl.multiple_of(qi * FOX_TQ, FOX_TQ), FOX_TQ)
    q_aug = [jnp.concatenate([q_ref[:, hs], qbias[q_rows, hs]], axis=-1) for hs in heads]
    ones_ext = jnp.ones((FOX_TK, HEAD_DIM), BF16)
    acc_ref[...] = jnp.zeros_like(acc_ref)

    def logits_into(kb, slot):
        ks = pl.ds(pl.multiple_of(kb * FOX_TK, FOX_TK), FOX_TK)
        for h, hs in enumerate(heads):
            k_aug = jnp.concatenate([k_ref[ks, hs], kbias[ks, hs]], axis=-1)
            u_ref[slot, h] = lax.dot_general(q_aug[h], k_aug, NT_DIMS, preferred_element_type=F32)

    def consume(kb, slot, m_old, diagonal):
        ks = pl.ds(pl.multiple_of(kb * FOX_TK, FOX_TK), FOX_TK)
        m_out = []
        for h, hs in enumerate(heads):
            u = u_ref[slot, h]
            if diagonal:
                u = jnp.where(_causal_mask(FOX_TQ), u, -jnp.inf)
            m_new = jnp.maximum(m_old[h], jnp.max(u, axis=-1, keepdims=True))
            p = jnp.exp2((u - m_new) * FOX_EXP2_SCALE)
            alpha = jnp.exp2((m_old[h] - m_new) * FOX_EXP2_SCALE)
            v_ext = jnp.concatenate([v_ref[ks, hs], ones_ext], axis=-1)
            acc_ref[h] = alpha * acc_ref[h] + jnp.dot(p.astype(BF16), v_ext,
                                                      preferred_element_type=F32)
            m_out.append(m_new)
        return tuple(m_out)

    def pair(i, m_old):
        logits_into(2 * i + 1, 1)
        m_mid = consume(2 * i, 0, m_old, False)
        logits_into(2 * i + 2, 0)
        return consume(2 * i + 1, 1, m_mid, False)

    logits_into(0, 0)
    m_init = tuple(jnp.full((FOX_TQ, 1), -jnp.inf, F32) for _ in heads)
    m_run = lax.fori_loop(0, qi // 2, pair, m_init)
    for h in range(FOX_HEADS):
        m_ref[h] = m_run[h]

    @pl.when(qi % 2 == 0)
    def _():
        consume(qi, 0, tuple(m_ref[h] for h in range(FOX_HEADS)), True)

    @pl.when(qi % 2 == 1)
    def _():
        logits_into(qi, 1)
        m_mid = consume(qi - 1, 0, tuple(m_ref[h] for h in range(FOX_HEADS)), False)
        consume(qi, 1, m_mid, True)

    for h, hs in enumerate(heads):
        acc = acc_ref[h]
        o = acc[:, :HEAD_DIM] / acc[:, HEAD_DIM:]
        y_ref[:, hs] = _rms_rows(o, hn_ref[:, hs]).astype(y_ref.dtype)


def _fox(main, gates, head_norm, batch, seq):
    nq = seq // FOX_TQ
    m = main.shape[0]
    sel = _fox_bias_selectors()
    return pl.pallas_call(
        functools.partial(_fox_kernel, seq=seq),
        grid=(batch, nq),
        in_specs=[
            pl.BlockSpec((FOX_TQ, D_FOX), lambda b, i: (b * nq + i, BLK_QA)),
            pl.BlockSpec((seq, D_FOX), lambda b, i: (b, BLK_KA)),
            pl.BlockSpec((seq, D_FOX), lambda b, i: (b, BLK_VA)),
            pl.BlockSpec((seq, LANES), lambda b, i: (b, 0)),
            pl.BlockSpec(sel.shape, lambda b, i: (0, 0)),
            pl.BlockSpec((1, D_FOX), lambda b, i: (0, 0)),
        ],
        out_specs=pl.BlockSpec((FOX_TQ, D_FOX), lambda b, i: (b * nq + i, 0)),
        out_shape=jax.ShapeDtypeStruct((m, D_FOX), BF16),
        scratch_shapes=[
            pltpu.VMEM((seq, D_FOX), BF16),
            pltpu.VMEM((seq, D_FOX), BF16),
            pltpu.VMEM((2, FOX_HEADS, FOX_TQ, FOX_TK), F32),
            pltpu.VMEM((FOX_HEADS, FOX_TQ, 1), F32),
            pltpu.VMEM((FOX_HEADS, FOX_TQ, 2 * HEAD_DIM), F32),
        ],
        compiler_params=pltpu.CompilerParams(
            dimension_semantics=("parallel", "arbitrary"),
            vmem_limit_bytes=VMEM_LIMIT),
        name="fox",
    )(main, main, main, gates, sel, head_norm)


CC_TS = 512
CC_HIST = 32
CC_ROWS = 32
CC_SHIFT_ROWS = CC_HIST + CC_TS - SUBLANES


def _cconv_kernel(u_ref, g_ref, w_ref, b_ref, lg_ref, lb_ref, y_ref, ybuf, yshift):
    si = pl.program_id(1)

    @pl.when(si == 0)
    def _():
        ybuf[0:CC_HIST, :] = jnp.zeros((CC_HIST, D_CONV), F32)

    @pl.when(si > 0)
    def _():
        ybuf[0:CC_HIST, :] = ybuf[CC_TS:CC_TS + CC_HIST, :]

    ybuf[CC_HIST:CC_HIST + CC_TS, :] = (
        u_ref[...].astype(F32) * _sigmoid(g_ref[...].astype(F32)))

    first = CC_HIST - (CONV_WIDTH - 1)
    for phase in range(1, SUBLANES):
        yshift[phase - 1] = ybuf[phase:phase + CC_SHIFT_ROWS, :]
    for r in range(CC_TS // CC_ROWS):
        acc = jnp.broadcast_to(b_ref[...], (CC_ROWS, D_CONV))
        for j in range(CONV_WIDTH):
            phase = (first + j) % SUBLANES
            lo = r * CC_ROWS + (first + j) - phase
            if phase == 0:
                tap = ybuf[lo:lo + CC_ROWS, :]
            else:
                tap = yshift[phase - 1, lo:lo + CC_ROWS, :]
            acc = acc + w_ref[j:j + 1, :] * tap
        mu = jnp.mean(acc, axis=-1, keepdims=True)
        xc = acc - mu
        yn = xc * lax.rsqrt(jnp.mean(xc * xc, axis=-1, keepdims=True) + EPS)
        yn = yn * lg_ref[...] + lb_ref[...]
        y_ref[r * CC_ROWS:(r + 1) * CC_ROWS, :] = _swish(yn).astype(y_ref.dtype)


def _cconv(main, w, b, ln_g, ln_b, batch, seq):
    ns = seq // CC_TS
    m = main.shape[0]
    return pl.pallas_call(
        _cconv_kernel,
        grid=(batch, ns),
        in_specs=[
            pl.BlockSpec((CC_TS, D_CONV), lambda bi, s: (bi * ns + s, BLK_UC)),
            pl.BlockSpec((CC_TS, D_CONV), lambda bi, s: (bi * ns + s, BLK_GC)),
            pl.BlockSpec((CONV_WIDTH, D_CONV), lambda bi, s: (0, 0)),
            pl.BlockSpec((1, D_CONV), lambda bi, s: (0, 0)),
            pl.BlockSpec((1, D_CONV), lambda bi, s: (0, 0)),
            pl.BlockSpec((1, D_CONV), lambda bi, s: (0, 0)),
        ],
        out_specs=pl.BlockSpec((CC_TS, D_CONV), lambda bi, s: (bi * ns + s, 0)),
        out_shape=jax.ShapeDtypeStruct((m, D_CONV), BF16),
        scratch_shapes=[
            pltpu.VMEM((CC_HIST + CC_TS, D_CONV), F32),
            pltpu.VMEM((SUBLANES - 1, CC_SHIFT_ROWS, D_CONV), F32),
        ],
        compiler_params=pltpu.CompilerParams(
            dimension_semantics=("parallel", "arbitrary"),
            vmem_limit_bytes=VMEM_LIMIT),
        name="cconv",
    )(main, main, w, b, ln_g, ln_b)


OUT_TM = 512
OUT_TN = 512


def _out_proj_kernel(x_ref, ym_ref, ya_ref, yc_ref, w_ref, o_ref):
    for n in range(D_MODEL // OUT_TN):
        cols = slice(n * OUT_TN, (n + 1) * OUT_TN)
        acc = jnp.dot(ym_ref[...], w_ref[0:D_MLSTM, cols], preferred_element_type=F32)
        acc = acc + jnp.dot(ya_ref[...], w_ref[D_MLSTM:D_MLSTM + D_FOX, cols],
                            preferred_element_type=F32)
        acc = acc + jnp.dot(yc_ref[...], w_ref[D_MLSTM + D_FOX:D_MODEL, cols],
                            preferred_element_type=F32)
        o_ref[:, cols] = x_ref[:, cols] + acc


def _out_proj(x2, ym, ya, yc, w_out, layer):
    m = x2.shape[0]
    return pl.pallas_call(
        _out_proj_kernel,
        grid=(m // OUT_TM,),
        in_specs=[
            pl.BlockSpec((OUT_TM, D_MODEL), lambda i: (i, 0)),
            pl.BlockSpec((OUT_TM, D_MLSTM), lambda i: (i, 0)),
            pl.BlockSpec((OUT_TM, D_FOX), lambda i: (i, 0)),
            pl.BlockSpec((OUT_TM, D_CONV), lambda i: (i, 0)),
            pl.BlockSpec((None, D_MODEL, D_MODEL), lambda i: (layer, 0, 0),
                         pipeline_mode=pl.Buffered(1)),
        ],
        out_specs=pl.BlockSpec((OUT_TM, D_MODEL), lambda i: (i, 0)),
        out_shape=jax.ShapeDtypeStruct((m, D_MODEL), F32),
        compiler_params=pltpu.CompilerParams(
            dimension_semantics=("parallel",),
            vmem_limit_bytes=VMEM_LIMIT),
        name="out_proj",
    )(x2, ym, ya, yc, w_out)


FFN_TM = 1024
FFN_TF = 512
FFN_SLAB = 128


def _ffn_kernel(x_ref, g_ref, wu_ref, wd_ref, fg_ref, o_ref, xn_ref, *, final_norm):
    j = pl.program_id(1)

    @pl.when(j == 0)
    def _():
        for r in range(FFN_TM // FFN_SLAB):
            rows = slice(r * FFN_SLAB, (r + 1) * FFN_SLAB)
            x = x_ref[rows, :]
            xn_ref[rows, :] = _rms_rows(x, g_ref[...]).astype(BF16)
            o_ref[rows, :] = x

    hid = jnp.maximum(jnp.dot(xn_ref[...], wu_ref[...], preferred_element_type=F32), 0.0)
    hid = (hid * hid).astype(BF16)
    o_ref[...] += jnp.dot(hid, wd_ref[...], preferred_element_type=F32)

    if final_norm:
        @pl.when(j == pl.num_programs(1) - 1)
        def _():
            for r in range(FFN_TM // FFN_SLAB):
                rows = slice(r * FFN_SLAB, (r + 1) * FFN_SLAB)
                o_ref[rows, :] = _rms_rows(o_ref[rows, :], fg_ref[...])


def _ffn(x2, g, w_up, w_down, final_g, layer, final_norm):
    m = x2.shape[0]
    return pl.pallas_call(
        functools.partial(_ffn_kernel, final_norm=final_norm),
        grid=(m // FFN_TM, D_FF // FFN_TF),
        in_specs=[
            pl.BlockSpec((FFN_TM, D_MODEL), lambda i, j: (i, 0)),
            pl.BlockSpec((1, D_MODEL), lambda i, j: (0, 0)),
            pl.BlockSpec((None, D_MODEL, FFN_TF), lambda i, j: (layer, 0, j)),
            pl.BlockSpec((None, FFN_TF, D_MODEL), lambda i, j: (layer, j, 0)),
            pl.BlockSpec((1, D_MODEL), lambda i, j: (0, 0)),
        ],
        out_specs=pl.BlockSpec((FFN_TM, D_MODEL), lambda i, j: (i, 0)),
        out_shape=jax.ShapeDtypeStruct((m, D_MODEL), F32),
        scratch_shapes=[pltpu.VMEM((FFN_TM, D_MODEL), BF16)],
        compiler_params=pltpu.CompilerParams(
            dimension_semantics=("parallel", "arbitrary"),
            vmem_limit_bytes=VMEM_LIMIT),
        name="ffn",
    )(x2, g, w_up, w_down, final_g)


OFF_GATE_M = 4 * D_MLSTM
OFF_FOX = OFF_GATE_M + 2 * MLSTM_HEADS
OFF_GATE_A = OFF_FOX + 3 * D_FOX
OFF_CONV = OFF_GATE_A + FOX_HEADS
D_IN = OFF_CONV + 2 * D_CONV
MAIN_FOX = 4 * D_MLSTM
MAIN_CONV = MAIN_FOX + 3 * D_FOX
assert OFF_GATE_M % LANES == 0 and (OFF_GATE_A - GATE_FA) % LANES == 0
PREP_ROWS = 256


def _prep_in_kernel(w_ref, main_ref, gate_ref):
    main_ref[:, 0:MAIN_FOX] = w_ref[:, 0:OFF_GATE_M].astype(BF16)
    main_ref[:, MAIN_FOX:MAIN_CONV] = w_ref[:, OFF_FOX:OFF_GATE_A].astype(BF16)
    main_ref[:, MAIN_CONV:D_MAIN] = w_ref[:, OFF_CONV:D_IN].astype(BF16)
    lane = lax.broadcasted_iota(jnp.int32, (1, LANES), 1)
    gate_m = w_ref[:, OFF_GATE_M:OFF_GATE_M + LANES]
    gate_a = w_ref[:, OFF_GATE_A - GATE_FA:OFF_GATE_A - GATE_FA + LANES]
    gate = jnp.where(lane < GATE_FA, gate_m, jnp.where(lane < GATE_FA + FOX_HEADS, gate_a, 0.0))
    gate_ref[...] = gate.astype(BF16)


def _prep_in_proj(w_in):
    depth, d, _ = w_in.shape
    return pl.pallas_call(
        _prep_in_kernel,
        grid=(depth, d // PREP_ROWS),
        in_specs=[pl.BlockSpec((None, PREP_ROWS, D_IN), lambda l, i: (l, i, 0))],
        out_specs=[
            pl.BlockSpec((None, PREP_ROWS, D_MAIN), lambda l, i: (l, i, 0)),
            pl.BlockSpec((None, PREP_ROWS, LANES), lambda l, i: (l, i, 0)),
        ],
        out_shape=[
            jax.ShapeDtypeStruct((depth, d, D_MAIN), BF16),
            jax.ShapeDtypeStruct((depth, d, LANES), BF16),
        ],
        compiler_params=pltpu.CompilerParams(
            dimension_semantics=("parallel", "parallel"),
            vmem_limit_bytes=VMEM_LIMIT),
        name="prep_in_proj",
    )(w_in)


def _split_bias(b_in):
    main = jnp.concatenate([b_in[..., 0:OFF_GATE_M], b_in[..., OFF_FOX:OFF_GATE_A],
                            b_in[..., OFF_CONV:D_IN]], axis=-1)
    pad = LANES - GATE_FA - FOX_HEADS
    gate = jnp.concatenate([b_in[..., OFF_GATE_M:OFF_FOX], b_in[..., OFF_GATE_A:OFF_CONV],
                            jnp.zeros(b_in.shape[:-1] + (pad,), b_in.dtype)], axis=-1)
    return main, gate


def kernel(x, norm_mix, w_in, b_in, mlstm_conv_w, mlstm_conv_b, mlstm_head_norm, fox_head_norm,
           conv_dw_w, conv_dw_b, conv_ln_g, conv_ln_b, w_out, norm_ffn, w_up, w_down, final_norm):
    batch, seq, d = x.shape
    assert d == D_MODEL and seq % max(FOX_TQ, IN_TM, CC_TS, MLSTM_STEP_CHUNKS * MLSTM_CHUNK) == 0
    depth = w_in.shape[0]
    x2 = x.reshape(batch * seq, d)
    assert w_in.shape[1:] == (D_MODEL, D_IN)
    w_main, w_gate = _prep_in_proj(w_in)
    b_main, b_gate = _split_bias(b_in)
    w_out_b = w_out.astype(BF16)
    w_up_b = w_up.astype(BF16)
    w_down_b = w_down.astype(BF16)
    row = lambda a: a.reshape(1, -1)

    for l in range(depth):
        main, gates = _in_proj(x2, row(norm_mix[l]), w_main, row(b_main[l]), w_gate, row(b_gate[l]), l)
        ym = _mlstm(main, gates, mlstm_conv_w[l], row(mlstm_conv_b[l]),
                    row(mlstm_head_norm[l]), batch, seq)
        ya = _fox(main, gates, row(fox_head_norm[l]), batch, seq)
        yc = _cconv(main, conv_dw_w[l], row(conv_dw_b[l]), row(conv_ln_g[l]),
                    row(conv_ln_b[l]), batch, seq)
        x2 = _out_proj(x2, ym, ya, yc, w_out_b, l)
        x2 = _ffn(x2, row(norm_ffn[l]), w_up_b, w_down_b, row(final_norm), l,
                  final_norm=(l == depth - 1))
    return x2.reshape(batch, seq, d)
```

```python
import functools
import math

import jax
import jax.numpy as jnp
import numpy as np
from jax import lax
from jax.experimental import pallas as pl
from jax.experimental.pallas import tpu as pltpu

D_MODEL = 2048
DEPTH = 4
HEAD_DIM = 128
D_MLSTM = D_MODEL // 2
D_FOX = D_MODEL // 4
D_CONV = D_MODEL - D_MLSTM - D_FOX
MLSTM_HEADS = D_MLSTM // HEAD_DIM
FOX_HEADS = D_FOX // HEAD_DIM
MLSTM_CONV_WIDTH = 4
CONV_WIDTH = 31
D_FF = 4 * D_MODEL
MLSTM_CHUNK = 128
EPS = 1e-6

LANES = 128
SUBLANES = 8
VMEM_LIMIT = 56 * 1024 * 1024

D_MAIN = 4 * D_MLSTM + 3 * D_FOX + 2 * D_CONV
GATE_I = 0
GATE_F = MLSTM_HEADS
GATE_FA = 2 * MLSTM_HEADS
BLK_QA, BLK_KA, BLK_VA, BLK_UC, BLK_GC = 8, 9, 10, 11, 12

BF16 = jnp.bfloat16
F32 = jnp.float32
NT_DIMS = (((1,), (1,)), ((), ()))
TN_DIMS = (((0,), (0,)), ((), ()))


def _rms_rows(x, g):
    ms = jnp.mean(x * x, axis=-1, keepdims=True)
    return x * lax.rsqrt(ms + EPS) * g


def _sigmoid(x):
    return 0.5 * jnp.tanh(0.5 * x) + 0.5


def _log_sigmoid(x):
    return jnp.minimum(x, 0.0) - jnp.log(1.0 + jnp.exp(-jnp.abs(x)))


def _swish(x, gain=1.0):
    return x * ((0.5 * gain) * jnp.tanh(0.5 * x) + (0.5 * gain))


def _causal_mask(n):
    r = lax.broadcasted_iota(jnp.int32, (n, n), 0)
    c = lax.broadcasted_iota(jnp.int32, (n, n), 1)
    return r >= c


def _split3(x):
    hi = x.astype(BF16)
    rem = x - hi.astype(F32)
    mid = rem.astype(BF16)
    lo = (rem - mid.astype(F32)).astype(BF16)
    return jnp.concatenate([hi, mid, lo], axis=-1)


def _cumsum_rows(x):
    n = x.shape[0]
    tri = jnp.where(_causal_mask(n), 1.0, 0.0).astype(BF16)
    y = jnp.dot(tri, _split3(x), preferred_element_type=F32)
    return y[:, :LANES] + y[:, LANES:2 * LANES] + y[:, 2 * LANES:]


IN_TM = 512
IN_TN = 512
IN_SLAB = 128


def _in_proj_kernel(x_ref, g_ref, w_ref, b_ref, wg_ref, bg_ref, out_ref, gates_ref, xn_ref):
    for r in range(IN_TM // IN_SLAB):
        rows = slice(r * IN_SLAB, (r + 1) * IN_SLAB)
        xn_ref[rows, :] = _rms_rows(x_ref[rows, :], g_ref[...]).astype(BF16)

    gates_ref[...] = jnp.dot(xn_ref[...], wg_ref[...], preferred_element_type=F32) + bg_ref[...]
    for n in range(D_MAIN // IN_TN):
        cols = slice(n * IN_TN, (n + 1) * IN_TN)
        acc = jnp.dot(xn_ref[...], w_ref[:, cols], preferred_element_type=F32)
        out_ref[:, cols] = (acc + b_ref[:, cols]).astype(out_ref.dtype)


def _in_proj(x2, g, w_main, b_main, w_gate, b_gate, layer):
    m = x2.shape[0]
    resident = pl.Buffered(1)
    return pl.pallas_call(
        _in_proj_kernel,
        grid=(m // IN_TM,),
        in_specs=[
            pl.BlockSpec((IN_TM, D_MODEL), lambda i: (i, 0)),
            pl.BlockSpec((1, D_MODEL), lambda i: (0, 0), pipeline_mode=resident),
            pl.BlockSpec((None, D_MODEL, D_MAIN), lambda i: (layer, 0, 0), pipeline_mode=resident),
            pl.BlockSpec((1, D_MAIN), lambda i: (0, 0), pipeline_mode=resident),
            pl.BlockSpec((None, D_MODEL, LANES), lambda i: (layer, 0, 0), pipeline_mode=resident),
            pl.BlockSpec((1, LANES), lambda i: (0, 0), pipeline_mode=resident),
        ],
        out_specs=[
            pl.BlockSpec((IN_TM, D_MAIN), lambda i: (i, 0)),
            pl.BlockSpec((IN_TM, LANES), lambda i: (i, 0)),
        ],
        out_shape=[
            jax.ShapeDtypeStruct((m, D_MAIN), BF16),
            jax.ShapeDtypeStruct((m, LANES), F32),
        ],
        scratch_shapes=[pltpu.VMEM((IN_TM, D_MODEL), BF16)],
        compiler_params=pltpu.CompilerParams(
            dimension_semantics=("parallel",),
            vmem_limit_bytes=VMEM_LIMIT),
        name="in_proj",
    )(x2, g, w_main, b_main, w_gate, b_gate)


CONV_PAD = SUBLANES
MLSTM_STEP_CHUNKS = 2


def _gate_prep(gates):
    bcum = _cumsum_rows(_log_sigmoid(gates))
    return gates, bcum, gates.T, bcum.T


def _mlstm_kernel(q_ref, k_ref, v_ref, o_ref, gt_ref, gt_next_ref, wq_ref, wk_ref, bq_ref, bk_ref,
                  hn_ref, y_ref, qbuf, kbuf, c_ref, m_ref, prep_ref):
    L = MLSTM_CHUNK
    step_rows = MLSTM_STEP_CHUNKS * L
    s = pl.program_id(1)
    step = pl.program_id(0) * pl.num_programs(1) + s
    slot = step % 2

    @pl.when(step == 0)
    def _():
        for ci in range(MLSTM_STEP_CHUNKS):
            for idx, tile in enumerate(_gate_prep(gt_ref[ci * L:(ci + 1) * L, :])):
                prep_ref[0, ci, idx] = tile

    @pl.when(s == 0)
    def _():
        qbuf[0:CONV_PAD, :] = jnp.zeros((CONV_PAD, D_MLSTM), F32)
        kbuf[0:CONV_PAD, :] = jnp.zeros((CONV_PAD, D_MLSTM), F32)
        c_ref[...] = jnp.zeros_like(c_ref)
        m_ref[...] = jnp.zeros_like(m_ref)

    @pl.when(s > 0)
    def _():
        qbuf[0:CONV_PAD, :] = qbuf[step_rows:step_rows + CONV_PAD, :]
        kbuf[0:CONV_PAD, :] = kbuf[step_rows:step_rows + CONV_PAD, :]

    qbuf[CONV_PAD:CONV_PAD + step_rows, :] = q_ref[...].astype(F32)
    kbuf[CONV_PAD:CONV_PAD + step_rows, :] = k_ref[...].astype(F32)

    causal = _causal_mask(L)
    ones_ext = jnp.ones((L, HEAD_DIM), BF16)

    def conv_swish(buf, w_ref, b_ref, lo, hs, gain):
        last = MLSTM_CONV_WIDTH - 1
        base = CONV_PAD + lo
        acc = b_ref[:, hs] + w_ref[last:last + 1, hs] * buf[base:base + L, hs]
        for j in range(last):
            off = base - last + j
            acc = acc + w_ref[j:j + 1, hs] * buf[off:off + L, hs]
        return _swish(acc, gain)

    for ci in range(MLSTM_STEP_CHUNKS):
        lo = ci * L
        rows = slice(lo, lo + L)
        gates = prep_ref[slot, ci, 0]
        bcum = prep_ref[slot, ci, 1]
        gates_t = prep_ref[slot, ci, 2]
        bcum_t = prep_ref[slot, ci, 3]

        for h in range(MLSTM_HEADS):
            hs = slice(h * HEAD_DIM, (h + 1) * HEAD_DIM)
            b_c = jnp.broadcast_to(bcum[:, GATE_F + h:GATE_F + h + 1], (L, L))
            i_c = jnp.broadcast_to(gates[:, GATE_I + h:GATE_I + h + 1], (L, L))
            b_r = bcum_t[GATE_F + h:GATE_F + h + 1, :]
            i_r = gates_t[GATE_I + h:GATE_I + h + 1, :]
            m_prev = m_ref[h, 0:1, :]

            d_log = jnp.where(causal, b_c - b_r + i_r, -jnp.inf)
            inter = b_c + m_prev
            m_t = jnp.maximum(inter, jnp.max(d_log, axis=-1, keepdims=True))
            w_intra = jnp.exp(d_log - m_t)
            w_inter = jnp.exp(inter - m_t)

            qh = conv_swish(qbuf, wq_ref, bq_ref, lo, hs, 1.0)
            kh = conv_swish(kbuf, wk_ref, bk_ref, lo, hs, HEAD_DIM ** -0.5)
            qb = qh.astype(BF16)
            kb = kh.astype(BF16)
            v_ext = jnp.concatenate([v_ref[rows, hs], ones_ext], axis=-1)
            c_ext = c_ref[h]

            sc = lax.dot_general(qb, kb, NT_DIMS, preferred_element_type=F32) * w_intra
            qc = jnp.dot(qb, c_ext.astype(BF16), preferred_element_type=F32)
            sv = jnp.dot(sc.astype(BF16), v_ext, preferred_element_type=F32)
            nx = jnp.concatenate([w_inter, w_inter], axis=-1) * qc + sv
            num = nx[:, :HEAD_DIM]
            den = nx[:, HEAD_DIM:]
            hh = num / jnp.maximum(jnp.abs(den), jnp.exp(-m_t))

            b_last = b_c[L - 1:L, :]
            g_c = b_last - b_c + i_c
            m_new = jnp.maximum(b_last + m_prev, jnp.max(g_c, axis=0, keepdims=True))
            w_k = jnp.exp(g_c - m_new)
            decay = jnp.exp(b_last + m_prev - m_new)
            kw = (kh * w_k).astype(BF16)
            upd = lax.dot_general(kw, v_ext, TN_DIMS, preferred_element_type=F32)
            c_ref[h] = jnp.concatenate([decay, decay], axis=-1) * c_ext + upd
            m_ref[h] = jnp.broadcast_to(m_new, (SUBLANES, LANES))

            hn = _rms_rows(hh, hn_ref[:, hs])
            y_ref[rows, hs] = (hn * _sigmoid(o_ref[rows, hs].astype(F32))).astype(y_ref.dtype)

    for ci in range(MLSTM_STEP_CHUNKS):
        for idx, tile in enumerate(_gate_prep(gt_next_ref[ci * L:(ci + 1) * L, :])):
            prep_ref[1 - slot, ci, idx] = tile


def _mlstm(main, gates, conv_w, conv_b, head_norm, batch, seq):
    L = MLSTM_CHUNK
    rows = MLSTM_STEP_CHUNKS * L
    ns = seq // rows
    m = main.shape[0]
    row = lambda b, s: b * ns + s
    return pl.pallas_call(
        _mlstm_kernel,
        grid=(batch, ns),
        in_specs=[
            pl.BlockSpec((rows, D_MLSTM), lambda b, s: (row(b, s), 0)),
            pl.BlockSpec((rows, D_MLSTM), lambda b, s: (row(b, s), 1)),
            pl.BlockSpec((rows, D_MLSTM), lambda b, s: (row(b, s), 2)),
            pl.BlockSpec((rows, D_MLSTM), lambda b, s: (row(b, s), 3)),
            pl.BlockSpec((rows, LANES), lambda b, s: (row(b, s), 0)),
            pl.BlockSpec((rows, LANES), lambda b, s: (jnp.minimum(row(b, s) + 1, m // rows - 1), 0)),
            pl.BlockSpec((MLSTM_CONV_WIDTH, D_MLSTM), lambda b, s: (0, 0)),
            pl.BlockSpec((MLSTM_CONV_WIDTH, D_MLSTM), lambda b, s: (0, 1)),
            pl.BlockSpec((1, D_MLSTM), lambda b, s: (0, 0)),
            pl.BlockSpec((1, D_MLSTM), lambda b, s: (0, 1)),
            pl.BlockSpec((1, D_MLSTM), lambda b, s: (0, 0)),
        ],
        out_specs=pl.BlockSpec((rows, D_MLSTM), lambda b, s: (row(b, s), 0)),
        out_shape=jax.ShapeDtypeStruct((m, D_MLSTM), BF16),
        scratch_shapes=[
            pltpu.VMEM((CONV_PAD + rows, D_MLSTM), F32),
            pltpu.VMEM((CONV_PAD + rows, D_MLSTM), F32),
            pltpu.VMEM((MLSTM_HEADS, HEAD_DIM, 2 * HEAD_DIM), F32),
            pltpu.VMEM((MLSTM_HEADS, SUBLANES, LANES), F32),
            pltpu.VMEM((2, MLSTM_STEP_CHUNKS, 4, L, LANES), F32),
        ],
        compiler_params=pltpu.CompilerParams(
            dimension_semantics=("arbitrary", "arbitrary"),
            vmem_limit_bytes=VMEM_LIMIT),
        name="mlstm",
    )(main, main, main, main, gates, gates, conv_w, conv_w, conv_b, conv_b, head_norm)


FOX_TQ = 256
FOX_TK = 256
FOX_SPLIT = 3
FOX_EXP2_SCALE = (HEAD_DIM ** -0.5) * math.log2(math.e)


def _fox_bias_selectors():
    sel = np.zeros((FOX_SPLIT * LANES, 2 * D_FOX), np.float32)
    for h in range(FOX_HEADS):
        for p in range(FOX_SPLIT):
            sel[p * LANES + GATE_FA + h, h * HEAD_DIM + p] = 1.0
            sel[p * LANES + GATE_FA + h, D_FOX + h * HEAD_DIM + FOX_SPLIT + p] = -1.0
    return jnp.asarray(sel, BF16)


def _fox_kernel(q_ref, k_ref, v_ref, gt_ref, sel_ref, hn_ref, y_ref, qbias, kbias, u_ref, acc_ref,
                *, seq):
    qi = pl.program_id(1)
    heads = [slice(h * HEAD_DIM, (h + 1) * HEAD_DIM) for h in range(FOX_HEADS)]

    @pl.when(qi == 0)
    def _():
        lane = lax.broadcasted_iota(jnp.int32, (1, 2 * D_FOX), 1)
        in_head = lane % HEAD_DIM
        ones_k = jnp.where(in_head < FOX_SPLIT, 1.0, 0.0)
        ones_q = jnp.where(in_head < 2 * FOX_SPLIT, 1.0, 0.0) - ones_k
        ones = jnp.where(lane >= D_FOX, ones_k, ones_q)
        carry = jnp.zeros((1, LANES), F32)
        for blk in range(seq // FOX_TK):
            rows = slice(blk * FOX_TK, (blk + 1) * FOX_TK)
            cs = _cumsum_rows(_log_sigmoid(gt_ref[rows, :])) + carry
            carry = cs[FOX_TK - 1:FOX_TK, :]
            bias = jnp.dot(_split3(cs * (HEAD_DIM ** 0.5)), sel_ref[...],
                           preferred_element_type=F32) + ones
            qbias[rows, :] = bias[:, :D_FOX].astype(BF16)
            kbias[rows, :] = bias[:, D_FOX:].astype(BF16)

    q_rows = pl.ds(pl.multiple_of(qi * FOX_TQ, FOX_TQ), FOX_TQ)
    q_aug = [jnp.concatenate([q_ref[:, hs], qbias[q_rows, hs]], axis=-1) for hs in heads]
    ones_ext = jnp.ones((FOX_TK, HEAD_DIM), BF16)
    acc_ref[...] = jnp.zeros_like(acc_ref)

    def logits_into(kb, slot):
        ks = pl.ds(pl.multiple_of(kb * FOX_TK, FOX_TK), FOX_TK)
        for h, hs in enumerate(heads):
            k_aug = jnp.concatenate([k_ref[ks, hs], kbias[ks, hs]], axis=-1)
            u_ref[slot, h] = lax.dot_general(q_aug[h], k_aug, NT_DIMS, preferred_element_type=F32)

    def consume(kb, slot, m_old, diagonal):
        ks = pl.ds(pl.multiple_of(kb * FOX_TK, FOX_TK), FOX_TK)
        m_out = []
        for h, hs in enumerate(heads):
            u = u_ref[slot, h]
            if diagonal:
                u = jnp.where(_causal_mask(FOX_TQ), u, -jnp.inf)
            m_new = jnp.maximum(m_old[h], jnp.max(u, axis=-1, keepdims=True))
            p = jnp.exp2((u - m_new) * FOX_EXP2_SCALE)
            alpha = jnp.exp2((m_old[h] - m_new) * FOX_EXP2_SCALE)
            v_ext = jnp.concatenate([v_ref[ks, hs], ones_ext], axis=-1)
            acc_ref[h] = alpha * acc_ref[h] + jnp.dot(p.astype(BF16), v_ext,
                                                      preferred_element_type=F32)
            m_out.append(m_new)
        return tuple(m_out)

    def body(kb, m_old):
        slot = kb % 2
        m_new = consume(kb, slot, m_old, False)
        logits_into(kb + 1, 1 - slot)
        return m_new

    logits_into(0, 0)
    m_init = tuple(jnp.full((FOX_TQ, 1), -jnp.inf, F32) for _ in heads)
    m_run = lax.fori_loop(0, qi, body, m_init)
    consume(qi, qi % 2, m_run, True)

    for h, hs in enumerate(heads):
        acc = acc_ref[h]
        o = acc[:, :HEAD_DIM] / acc[:, HEAD_DIM:]
        y_ref[:, hs] = _rms_rows(o, hn_ref[:, hs]).astype(y_ref.dtype)


def _fox(main, gates, head_norm, batch, seq):
    nq = seq // FOX_TQ
    m = main.shape[0]
    sel = _fox_bias_selectors()
    return pl.pallas_call(
        functools.partial(_fox_kernel, seq=seq),
        grid=(batch, nq),
        in_specs=[
            pl.BlockSpec((FOX_TQ, D_FOX), lambda b, i: (b * nq + i, BLK_QA)),
            pl.BlockSpec((seq, D_FOX), lambda b, i: (b, BLK_KA)),
            pl.BlockSpec((seq, D_FOX), lambda b, i: (b, BLK_VA)),
            pl.BlockSpec((seq, LANES), lambda b, i: (b, 0)),
            pl.BlockSpec(sel.shape, lambda b, i: (0, 0)),
            pl.BlockSpec((1, D_FOX), lambda b, i: (0, 0)),
        ],
        out_specs=pl.BlockSpec((FOX_TQ, D_FOX), lambda b, i: (b * nq + i, 0)),
        out_shape=jax.ShapeDtypeStruct((m, D_FOX), BF16),
        scratch_shapes=[
            pltpu.VMEM((seq, D_FOX), BF16),
            pltpu.VMEM((seq, D_FOX), BF16),
            pltpu.VMEM((2, FOX_HEADS, FOX_TQ, FOX_TK), F32),
            pltpu.VMEM((FOX_HEADS, FOX_TQ, 2 * HEAD_DIM), F32),
        ],
        compiler_params=pltpu.CompilerParams(
            dimension_semantics=("parallel", "arbitrary"),
            vmem_limit_bytes=VMEM_LIMIT),
        name="fox",
    )(main, main, main, gates, sel, head_norm)


CC_TS = 512
CC_HIST = 32
CC_ROWS = 32
CC_SHIFT_ROWS = CC_HIST + CC_TS - SUBLANES


def _cconv_kernel(u_ref, g_ref, w_ref, b_ref, lg_ref, lb_ref, y_ref, ybuf, yshift):
    si = pl.program_id(1)

    @pl.when(si == 0)
    def _():
        ybuf[0:CC_HIST, :] = jnp.zeros((CC_HIST, D_CONV), F32)

    @pl.when(si > 0)
    def _():
        ybuf[0:CC_HIST, :] = ybuf[CC_TS:CC_TS + CC_HIST, :]

    ybuf[CC_HIST:CC_HIST + CC_TS, :] = (
        u_ref[...].astype(F32) * _sigmoid(g_ref[...].astype(F32)))

    first = CC_HIST - (CONV_WIDTH - 1)
    for phase in range(1, SUBLANES):
        yshift[phase - 1] = ybuf[phase:phase + CC_SHIFT_ROWS, :]
    for r in range(CC_TS // CC_ROWS):
        acc = jnp.broadcast_to(b_ref[...], (CC_ROWS, D_CONV))
        for j in range(CONV_WIDTH):
            phase = (first + j) % SUBLANES
            lo = r * CC_ROWS + (first + j) - phase
            if phase == 0:
                tap = ybuf[lo:lo + CC_ROWS, :]
            else:
                tap = yshift[phase - 1, lo:lo + CC_ROWS, :]
            acc = acc + w_ref[j:j + 1, :] * tap
        mu = jnp.mean(acc, axis=-1, keepdims=True)
        xc = acc - mu
        yn = xc * lax.rsqrt(jnp.mean(xc * xc, axis=-1, keepdims=True) + EPS)
        yn = yn * lg_ref[...] + lb_ref[...]
        y_ref[r * CC_ROWS:(r + 1) * CC_ROWS, :] = _swish(yn).astype(y_ref.dtype)


def _cconv(main, w, b, ln_g, ln_b, batch, seq):
    ns = seq // CC_TS
    m = main.shape[0]
    return pl.pallas_call(
        _cconv_kernel,
        grid=(batch, ns),
        in_specs=[
            pl.BlockSpec((CC_TS, D_CONV), lambda bi, s: (bi * ns + s, BLK_UC)),
            pl.BlockSpec((CC_TS, D_CONV), lambda bi, s: (bi * ns + s, BLK_GC)),
            pl.BlockSpec((CONV_WIDTH, D_CONV), lambda bi, s: (0, 0)),
            pl.BlockSpec((1, D_CONV), lambda bi, s: (0, 0)),
            pl.BlockSpec((1, D_CONV), lambda bi, s: (0, 0)),
            pl.BlockSpec((1, D_CONV), lambda bi, s: (0, 0)),
        ],
        out_specs=pl.BlockSpec((CC_TS, D_CONV), lambda bi, s: (bi * ns + s, 0)),
        out_shape=jax.ShapeDtypeStruct((m, D_CONV), BF16),
        scratch_shapes=[
            pltpu.VMEM((CC_HIST + CC_TS, D_CONV), F32),
            pltpu.VMEM((SUBLANES - 1, CC_SHIFT_ROWS, D_CONV), F32),
        ],
        compiler_params=pltpu.CompilerParams(
            dimension_semantics=("parallel", "arbitrary"),
            vmem_limit_bytes=VMEM_LIMIT),
        name="cconv",
    )(main, main, w, b, ln_g, ln_b)


OUT_TM = 512
OUT_TN = 512


def _out_proj_kernel(x_ref, ym_ref, ya_ref, yc_ref, w_ref, o_ref):
    for n in range(D_MODEL // OUT_TN):
        cols = slice(n * OUT_TN, (n + 1) * OUT_TN)
        acc = jnp.dot(ym_ref[...], w_ref[0:D_MLSTM, cols], preferred_element_type=F32)
        acc = acc + jnp.dot(ya_ref[...], w_ref[D_MLSTM:D_MLSTM + D_FOX, cols],
                            preferred_element_type=F32)
        acc = acc + jnp.dot(yc_ref[...], w_ref[D_MLSTM + D_FOX:D_MODEL, cols],
                            preferred_element_type=F32)
        o_ref[:, cols] = x_ref[:, cols] + acc


def _out_proj(x2, ym, ya, yc, w_out, layer):
    m = x2.shape[0]
    return pl.pallas_call(
        _out_proj_kernel,
        grid=(m // OUT_TM,),
        in_specs=[
            pl.BlockSpec((OUT_TM, D_MODEL), lambda i: (i, 0)),
            pl.BlockSpec((OUT_TM, D_MLSTM), lambda i: (i, 0)),
            pl.BlockSpec((OUT_TM, D_FOX), lambda i: (i, 0)),
            pl.BlockSpec((OUT_TM, D_CONV), lambda i: (i, 0)),
            pl.BlockSpec((None, D_MODEL, D_MODEL), lambda i: (layer, 0, 0),
                         pipeline_mode=pl.Buffered(1)),
        ],
        out_specs=pl.BlockSpec((OUT_TM, D_MODEL), lambda i: (i, 0)),
        out_shape=jax.ShapeDtypeStruct((m, D_MODEL), F32),
        compiler_params=pltpu.CompilerParams(
            dimension_semantics=("parallel",),
            vmem_limit_bytes=VMEM_LIMIT),
        name="out_proj",
    )(x2, ym, ya, yc, w_out)


FFN_TM = 1024
FFN_TF = 1024
FFN_HALF = 512
FFN_SLAB = 128
FFN_VMEM_LIMIT = 60 * 1024 * 1024


def _ffn_kernel(x_ref, g_ref, wu_ref, wd_ref, fg_ref, o_ref, xn_ref, *, final_norm):
    j = pl.program_id(1)

    @pl.when(j == 0)
    def _():
        for r in range(FFN_TM // FFN_SLAB):
            rows = slice(r * FFN_SLAB, (r + 1) * FFN_SLAB)
            x = x_ref[rows, :]
            xn_ref[rows, :] = _rms_rows(x, g_ref[...]).astype(BF16)
            o_ref[rows, :] = x

    for part in range(FFN_TF // FFN_HALF):
        cols = slice(part * FFN_HALF, (part + 1) * FFN_HALF)
        hid = jnp.maximum(jnp.dot(xn_ref[...], wu_ref[:, cols], preferred_element_type=F32), 0.0)
        hid = (hid * hid).astype(BF16)
        o_ref[...] += jnp.dot(hid, wd_ref[cols, :], preferred_element_type=F32)

    if final_norm:
        @pl.when(j == pl.num_programs(1) - 1)
        def _():
            for r in range(FFN_TM // FFN_SLAB):
                rows = slice(r * FFN_SLAB, (r + 1) * FFN_SLAB)
                o_ref[rows, :] = _rms_rows(o_ref[rows, :], fg_ref[...])


def _ffn(x2, g, w_up, w_down, final_g, layer, final_norm):
    m = x2.shape[0]
    return pl.pallas_call(
        functools.partial(_ffn_kernel, final_norm=final_norm),
        grid=(m // FFN_TM, D_FF // FFN_TF),
        in_specs=[
            pl.BlockSpec((FFN_TM, D_MODEL), lambda i, j: (i, 0)),
            pl.BlockSpec((1, D_MODEL), lambda i, j: (0, 0)),
            pl.BlockSpec((None, D_MODEL, FFN_TF), lambda i, j: (layer, 0, j)),
            pl.BlockSpec((None, FFN_TF, D_MODEL), lambda i, j: (layer, j, 0)),
            pl.BlockSpec((1, D_MODEL), lambda i, j: (0, 0)),
        ],
        out_specs=pl.BlockSpec((FFN_TM, D_MODEL), lambda i, j: (i, 0)),
        out_shape=jax.ShapeDtypeStruct((m, D_MODEL), F32),
        scratch_shapes=[pltpu.VMEM((FFN_TM, D_MODEL), BF16)],
        compiler_params=pltpu.CompilerParams(
            dimension_semantics=("parallel", "arbitrary"),
            vmem_limit_bytes=FFN_VMEM_LIMIT),
        name="ffn",
    )(x2, g, w_up, w_down, final_g)


OFF_GATE_M = 4 * D_MLSTM
OFF_FOX = OFF_GATE_M + 2 * MLSTM_HEADS
OFF_GATE_A = OFF_FOX + 3 * D_FOX
OFF_CONV = OFF_GATE_A + FOX_HEADS
D_IN = OFF_CONV + 2 * D_CONV
MAIN_FOX = 4 * D_MLSTM
MAIN_CONV = MAIN_FOX + 3 * D_FOX
assert OFF_GATE_M % LANES == 0 and (OFF_GATE_A - GATE_FA) % LANES == 0
PREP_ROWS = 256


def _prep_in_kernel(w_ref, main_ref, gate_ref):
    main_ref[:, 0:MAIN_FOX] = w_ref[:, 0:OFF_GATE_M].astype(BF16)
    main_ref[:, MAIN_FOX:MAIN_CONV] = w_ref[:, OFF_FOX:OFF_GATE_A].astype(BF16)
    main_ref[:, MAIN_CONV:D_MAIN] = w_ref[:, OFF_CONV:D_IN].astype(BF16)
    lane = lax.broadcasted_iota(jnp.int32, (1, LANES), 1)
    gate_m = w_ref[:, OFF_GATE_M:OFF_GATE_M + LANES]
    gate_a = w_ref[:, OFF_GATE_A - GATE_FA:OFF_GATE_A - GATE_FA + LANES]
    gate = jnp.where(lane < GATE_FA, gate_m, jnp.where(lane < GATE_FA + FOX_HEADS, gate_a, 0.0))
    gate_ref[...] = gate.astype(BF16)


def _prep_in_proj(w_in):
    depth, d, _ = w_in.shape
    return pl.pallas_call(
        _prep_in_kernel,
        grid=(depth, d // PREP_ROWS),
        in_specs=[pl.BlockSpec((None, PREP_ROWS, D_IN), lambda l, i: (l, i, 0))],
        out_specs=[
            pl.BlockSpec((None, PREP_ROWS, D_MAIN), lambda l, i: (l, i, 0)),
            pl.BlockSpec((None, PREP_ROWS, LANES), lambda l, i: (l, i, 0)),
        ],
        out_shape=[
            jax.ShapeDtypeStruct((depth, d, D_MAIN), BF16),
            jax.ShapeDtypeStruct((depth, d, LANES), BF16),
        ],
        compiler_params=pltpu.CompilerParams(
            dimension_semantics=("parallel", "parallel"),
            vmem_limit_bytes=VMEM_LIMIT),
        name="prep_in_proj",
    )(w_in)


def _split_bias(b_in):
    main = jnp.concatenate([b_in[..., 0:OFF_GATE_M], b_in[..., OFF_FOX:OFF_GATE_A],
                            b_in[..., OFF_CONV:D_IN]], axis=-1)
    pad = LANES - GATE_FA - FOX_HEADS
    gate = jnp.concatenate([b_in[..., OFF_GATE_M:OFF_FOX], b_in[..., OFF_GATE_A:OFF_CONV],
                            jnp.zeros(b_in.shape[:-1] + (pad,), b_in.dtype)], axis=-1)
    return main, gate


def kernel(x, norm_mix, w_in, b_in, mlstm_conv_w, mlstm_conv_b, mlstm_head_norm, fox_head_norm,
           conv_dw_w, conv_dw_b, conv_ln_g, conv_ln_b, w_out, norm_ffn, w_up, w_down, final_norm):
    batch, seq, d = x.shape
    assert d == D_MODEL and seq % max(FOX_TQ, IN_TM, CC_TS, MLSTM_STEP_CHUNKS * MLSTM_CHUNK) == 0
    depth = w_in.shape[0]
    x2 = x.reshape(batch * seq, d)
    assert w_in.shape[1:] == (D_MODEL, D_IN)
    w_main, w_gate = _prep_in_proj(w_in)
    b_main, b_gate = _split_bias(b_in)
    w_out_b = w_out.astype(BF16)
    w_up_b = w_up.astype(BF16)
    w_down_b = w_down.astype(BF16)
    row = lambda a: a.reshape(1, -1)

    for l in range(depth):
        main, gates = _in_proj(x2, row(norm_mix[l]), w_main, row(b_main[l]), w_gate, row(b_gate[l]), l)
        ym = _mlstm(main, gates, mlstm_conv_w[l], row(mlstm_conv_b[l]),
                    row(mlstm_head_norm[l]), batch, seq)
        ya = _fox(main, gates, row(fox_head_norm[l]), batch, seq)
        yc = _cconv(main, conv_dw_w[l], row(conv_dw_b[l]), row(conv_ln_g[l]),
                    row(conv_ln_b[l]), batch, seq)
        x2 = _out_proj(x2, ym, ya, yc, w_out_b, l)
        x2 = _ffn(x2, row(norm_ffn[l]), w_up_b, w_down_b, row(final_norm), l,
                  final_norm=(l == depth - 1))
    return x2.reshape(batch, seq, d)
```

```python
import functools
import math

import jax
import jax.numpy as jnp
import numpy as np
from jax import lax
from jax.experimental import pallas as pl
from jax.experimental.pallas import tpu as pltpu

D_MODEL = 2048
DEPTH = 4
HEAD_DIM = 128
D_MLSTM = D_MODEL // 2
D_FOX = D_MODEL // 4
D_CONV = D_MODEL - D_MLSTM - D_FOX
MLSTM_HEADS = D_MLSTM // HEAD_DIM
FOX_HEADS = D_FOX // HEAD_DIM
MLSTM_CONV_WIDTH = 4
CONV_WIDTH = 31
D_FF = 4 * D_MODEL
MLSTM_CHUNK = 128
EPS = 1e-6

LANES = 128
SUBLANES = 8
VMEM_LIMIT = 56 * 1024 * 1024

D_MAIN = 4 * D_MLSTM + 3 * D_FOX + 2 * D_CONV
GATE_I = 0
GATE_F = MLSTM_HEADS
GATE_FA = 2 * MLSTM_HEADS
BLK_QA, BLK_KA, BLK_VA, BLK_UC, BLK_GC = 8, 9, 10, 11, 12

BF16 = jnp.bfloat16
F32 = jnp.float32
NT_DIMS = (((1,), (1,)), ((), ()))
TN_DIMS = (((0,), (0,)), ((), ()))


def _rms_rows(x, g):
    ms = jnp.mean(x * x, axis=-1, keepdims=True)
    return x * lax.rsqrt(ms + EPS) * g


def _sigmoid(x):
    return 0.5 * jnp.tanh(0.5 * x) + 0.5


def _log_sigmoid(x):
    return jnp.minimum(x, 0.0) - jnp.log(1.0 + jnp.exp(-jnp.abs(x)))


def _swish(x, gain=1.0):
    return x * ((0.5 * gain) * jnp.tanh(0.5 * x) + (0.5 * gain))


def _causal_mask(n):
    r = lax.broadcasted_iota(jnp.int32, (n, n), 0)
    c = lax.broadcasted_iota(jnp.int32, (n, n), 1)
    return r >= c


def _split3(x):
    hi = x.astype(BF16)
    rem = x - hi.astype(F32)
    mid = rem.astype(BF16)
    lo = (rem - mid.astype(F32)).astype(BF16)
    return jnp.concatenate([hi, mid, lo], axis=-1)


def _cumsum_rows(x):
    n = x.shape[0]
    tri = jnp.where(_causal_mask(n), 1.0, 0.0).astype(BF16)
    y = jnp.dot(tri, _split3(x), preferred_element_type=F32)
    return y[:, :LANES] + y[:, LANES:2 * LANES] + y[:, 2 * LANES:]


IN_TM = 512
IN_TN = 512
IN_SLAB = 128


def _in_proj_kernel(x_ref, g_ref, w_ref, b_ref, wg_ref, bg_ref, out_ref, gates_ref, xn_ref):
    for r in range(IN_TM // IN_SLAB):
        rows = slice(r * IN_SLAB, (r + 1) * IN_SLAB)
        xn_ref[rows, :] = _rms_rows(x_ref[rows, :], g_ref[...]).astype(BF16)

    gates_ref[...] = jnp.dot(xn_ref[...], wg_ref[...], preferred_element_type=F32) + bg_ref[...]
    for n in range(D_MAIN // IN_TN):
        cols = slice(n * IN_TN, (n + 1) * IN_TN)
        acc = jnp.dot(xn_ref[...], w_ref[:, cols], preferred_element_type=F32)
        out_ref[:, cols] = (acc + b_ref[:, cols]).astype(out_ref.dtype)


def _in_proj(x2, g, w_main, b_main, w_gate, b_gate, layer):
    m = x2.shape[0]
    resident = pl.Buffered(1)
    return pl.pallas_call(
        _in_proj_kernel,
        grid=(m // IN_TM,),
        in_specs=[
            pl.BlockSpec((IN_TM, D_MODEL), lambda i: (i, 0)),
            pl.BlockSpec((1, D_MODEL), lambda i: (0, 0), pipeline_mode=resident),
            pl.BlockSpec((None, D_MODEL, D_MAIN), lambda i: (layer, 0, 0), pipeline_mode=resident),
            pl.BlockSpec((1, D_MAIN), lambda i: (0, 0), pipeline_mode=resident),
            pl.BlockSpec((None, D_MODEL, LANES), lambda i: (layer, 0, 0), pipeline_mode=resident),
            pl.BlockSpec((1, LANES), lambda i: (0, 0), pipeline_mode=resident),
        ],
        out_specs=[
            pl.BlockSpec((IN_TM, D_MAIN), lambda i: (i, 0)),
            pl.BlockSpec((IN_TM, LANES), lambda i: (i, 0)),
        ],
        out_shape=[
            jax.ShapeDtypeStruct((m, D_MAIN), BF16),
            jax.ShapeDtypeStruct((m, LANES), F32),
        ],
        scratch_shapes=[pltpu.VMEM((IN_TM, D_MODEL), BF16)],
        compiler_params=pltpu.CompilerParams(
            dimension_semantics=("parallel",),
            vmem_limit_bytes=VMEM_LIMIT),
        name="in_proj",
    )(x2, g, w_main, b_main, w_gate, b_gate)


CONV_PAD = SUBLANES
MLSTM_STEP_CHUNKS = 4


def _gate_prep(gates):
    bcum = _cumsum_rows(_log_sigmoid(gates))
    return gates, bcum, gates.T, bcum.T


def _mlstm_kernel(q_ref, k_ref, v_ref, o_ref, gt_ref, gt_next_ref, wq_ref, wk_ref, bq_ref, bk_ref,
                  hn_ref, y_ref, qbuf, kbuf, c_ref, m_ref, prep_ref):
    L = MLSTM_CHUNK
    step_rows = MLSTM_STEP_CHUNKS * L
    s = pl.program_id(1)
    step = pl.program_id(0) * pl.num_programs(1) + s
    slot = step % 2

    @pl.when(step == 0)
    def _():
        for ci in range(MLSTM_STEP_CHUNKS):
            for idx, tile in enumerate(_gate_prep(gt_ref[ci * L:(ci + 1) * L, :])):
                prep_ref[0, ci, idx] = tile

    @pl.when(s == 0)
    def _():
        qbuf[0:CONV_PAD, :] = jnp.zeros((CONV_PAD, D_MLSTM), F32)
        kbuf[0:CONV_PAD, :] = jnp.zeros((CONV_PAD, D_MLSTM), F32)
        c_ref[...] = jnp.zeros_like(c_ref)
        m_ref[...] = jnp.zeros_like(m_ref)

    @pl.when(s > 0)
    def _():
        qbuf[0:CONV_PAD, :] = qbuf[step_rows:step_rows + CONV_PAD, :]
        kbuf[0:CONV_PAD, :] = kbuf[step_rows:step_rows + CONV_PAD, :]

    qbuf[CONV_PAD:CONV_PAD + step_rows, :] = q_ref[...].astype(F32)
    kbuf[CONV_PAD:CONV_PAD + step_rows, :] = k_ref[...].astype(F32)

    causal = _causal_mask(L)
    ones_ext = jnp.ones((L, HEAD_DIM), BF16)

    def conv_swish(buf, w_ref, b_ref, lo, hs, gain):
        last = MLSTM_CONV_WIDTH - 1
        base = CONV_PAD + lo
        acc = b_ref[:, hs] + w_ref[last:last + 1, hs] * buf[base:base + L, hs]
        for j in range(last):
            off = base - last + j
            acc = acc + w_ref[j:j + 1, hs] * buf[off:off + L, hs]
        return _swish(acc, gain)

    for ci in range(MLSTM_STEP_CHUNKS):
        lo = ci * L
        rows = slice(lo, lo + L)
        gates = prep_ref[slot, ci, 0]
        bcum = prep_ref[slot, ci, 1]
        gates_t = prep_ref[slot, ci, 2]
        bcum_t = prep_ref[slot, ci, 3]

        for h in range(MLSTM_HEADS):
            hs = slice(h * HEAD_DIM, (h + 1) * HEAD_DIM)
            b_c = jnp.broadcast_to(bcum[:, GATE_F + h:GATE_F + h + 1], (L, L))
            i_c = jnp.broadcast_to(gates[:, GATE_I + h:GATE_I + h + 1], (L, L))
            b_r = bcum_t[GATE_F + h:GATE_F + h + 1, :]
            i_r = gates_t[GATE_I + h:GATE_I + h + 1, :]
            m_prev = m_ref[h, 0:1, :]

            d_log = jnp.where(causal, b_c - b_r + i_r, -jnp.inf)
            inter = b_c + m_prev
            m_t = jnp.maximum(inter, jnp.max(d_log, axis=-1, keepdims=True))
            w_intra = jnp.exp(d_log - m_t)
            w_inter = jnp.exp(inter - m_t)

            qh = conv_swish(qbuf, wq_ref, bq_ref, lo, hs, 1.0)
            kh = conv_swish(kbuf, wk_ref, bk_ref, lo, hs, HEAD_DIM ** -0.5)
            qb = qh.astype(BF16)
            kb = kh.astype(BF16)
            v_ext = jnp.concatenate([v_ref[rows, hs], ones_ext], axis=-1)
            c_ext = c_ref[h]

            sc = lax.dot_general(qb, kb, NT_DIMS, preferred_element_type=F32) * w_intra
            qc = jnp.dot(qb, c_ext.astype(BF16), preferred_element_type=F32)
            sv = jnp.dot(sc.astype(BF16), v_ext, preferred_element_type=F32)
            nx = jnp.concatenate([w_inter, w_inter], axis=-1) * qc + sv
            num = nx[:, :HEAD_DIM]
            den = nx[:, HEAD_DIM:]
            hh = num / jnp.maximum(jnp.abs(den), jnp.exp(-m_t))

            b_last = b_c[L - 1:L, :]
            g_c = b_last - b_c + i_c
            m_new = jnp.maximum(b_last + m_prev, jnp.max(g_c, axis=0, keepdims=True))
            w_k = jnp.exp(g_c - m_new)
            decay = jnp.exp(b_last + m_prev - m_new)
            kw = (kh * w_k).astype(BF16)
            upd = lax.dot_general(kw, v_ext, TN_DIMS, preferred_element_type=F32)
            c_ref[h] = jnp.concatenate([decay, decay], axis=-1) * c_ext + upd
            m_ref[h] = jnp.broadcast_to(m_new, (SUBLANES, LANES))

            hn = _rms_rows(hh, hn_ref[:, hs])
            y_ref[rows, hs] = (hn * _sigmoid(o_ref[rows, hs].astype(F32))).astype(y_ref.dtype)

    for ci in range(MLSTM_STEP_CHUNKS):
        for idx, tile in enumerate(_gate_prep(gt_next_ref[ci * L:(ci + 1) * L, :])):
            prep_ref[1 - slot, ci, idx] = tile


def _mlstm(main, gates, conv_w, conv_b, head_norm, batch, seq):
    L = MLSTM_CHUNK
    rows = MLSTM_STEP_CHUNKS * L
    ns = seq // rows
    m = main.shape[0]
    row = lambda b, s: b * ns + s
    return pl.pallas_call(
        _mlstm_kernel,
        grid=(batch, ns),
        in_specs=[
            pl.BlockSpec((rows, D_MLSTM), lambda b, s: (row(b, s), 0)),
            pl.BlockSpec((rows, D_MLSTM), lambda b, s: (row(b, s), 1)),
            pl.BlockSpec((rows, D_MLSTM), lambda b, s: (row(b, s), 2)),
            pl.BlockSpec((rows, D_MLSTM), lambda b, s: (row(b, s), 3)),
            pl.BlockSpec((rows, LANES), lambda b, s: (row(b, s), 0)),
            pl.BlockSpec((rows, LANES), lambda b, s: (jnp.minimum(row(b, s) + 1, m // rows - 1), 0)),
            pl.BlockSpec((MLSTM_CONV_WIDTH, D_MLSTM), lambda b, s: (0, 0)),
            pl.BlockSpec((MLSTM_CONV_WIDTH, D_MLSTM), lambda b, s: (0, 1)),
            pl.BlockSpec((1, D_MLSTM), lambda b, s: (0, 0)),
            pl.BlockSpec((1, D_MLSTM), lambda b, s: (0, 1)),
            pl.BlockSpec((1, D_MLSTM), lambda b, s: (0, 0)),
        ],
        out_specs=pl.BlockSpec((rows, D_MLSTM), lambda b, s: (row(b, s), 0)),
        out_shape=jax.ShapeDtypeStruct((m, D_MLSTM), BF16),
        scratch_shapes=[
            pltpu.VMEM((CONV_PAD + rows, D_MLSTM), F32),
            pltpu.VMEM((CONV_PAD + rows, D_MLSTM), F32),
            pltpu.VMEM((MLSTM_HEADS, HEAD_DIM, 2 * HEAD_DIM), F32),
            pltpu.VMEM((MLSTM_HEADS, SUBLANES, LANES), F32),
            pltpu.VMEM((2, MLSTM_STEP_CHUNKS, 4, L, LANES), F32),
        ],
        compiler_params=pltpu.CompilerParams(
            dimension_semantics=("arbitrary", "arbitrary"),
            vmem_limit_bytes=VMEM_LIMIT),
        name="mlstm",
    )(main, main, main, main, gates, gates, conv_w, conv_w, conv_b, conv_b, head_norm)


FOX_TQ = 256
FOX_TK = 256
FOX_SPLIT = 3
FOX_EXP2_SCALE = (HEAD_DIM ** -0.5) * math.log2(math.e)


def _fox_bias_selectors():
    sel = np.zeros((FOX_SPLIT * LANES, 2 * D_FOX), np.float32)
    for h in range(FOX_HEADS):
        for p in range(FOX_SPLIT):
            sel[p * LANES + GATE_FA + h, h * HEAD_DIM + p] = 1.0
            sel[p * LANES + GATE_FA + h, D_FOX + h * HEAD_DIM + FOX_SPLIT + p] = -1.0
    return jnp.asarray(sel, BF16)


def _fox_kernel(q_ref, k_ref, v_ref, gt_ref, sel_ref, hn_ref, y_ref, qbias, kbias, u_ref, acc_ref,
                *, seq):
    qi = pl.program_id(1)
    heads = [slice(h * HEAD_DIM, (h + 1) * HEAD_DIM) for h in range(FOX_HEADS)]

    @pl.when(qi == 0)
    def _():
        lane = lax.broadcasted_iota(jnp.int32, (1, 2 * D_FOX), 1)
        in_head = lane % HEAD_DIM
        ones_k = jnp.where(in_head < FOX_SPLIT, 1.0, 0.0)
        ones_q = jnp.where(in_head < 2 * FOX_SPLIT, 1.0, 0.0) - ones_k
        ones = jnp.where(lane >= D_FOX, ones_k, ones_q)
        carry = jnp.zeros((1, LANES), F32)
        for blk in range(seq // FOX_TK):
            rows = slice(blk * FOX_TK, (blk + 1) * FOX_TK)
            cs = _cumsum_rows(_log_sigmoid(gt_ref[rows, :])) + carry
            carry = cs[FOX_TK - 1:FOX_TK, :]
            bias = jnp.dot(_split3(cs * (HEAD_DIM ** 0.5)), sel_ref[...],
                           preferred_element_type=F32) + ones
            qbias[rows, :] = bias[:, :D_FOX].astype(BF16)
            kbias[rows, :] = bias[:, D_FOX:].astype(BF16)

    q_rows = pl.ds(pl.multiple_of(qi * FOX_TQ, FOX_TQ), FOX_TQ)
    q_aug = [jnp.concatenate([q_ref[:, hs], qbias[q_rows, hs]], axis=-1) for hs in heads]
    ones_ext = jnp.ones((FOX_TK, HEAD_DIM), BF16)
    acc_ref[...] = jnp.zeros_like(acc_ref)

    def logits_into(kb, slot):
        ks = pl.ds(pl.multiple_of(kb * FOX_TK, FOX_TK), FOX_TK)
        for h, hs in enumerate(heads):
            k_aug = jnp.concatenate([k_ref[ks, hs], kbias[ks, hs]], axis=-1)
            u_ref[slot, h] = lax.dot_general(q_aug[h], k_aug, NT_DIMS, preferred_element_type=F32)

    def consume(kb, slot, m_old, diagonal):
        ks = pl.ds(pl.multiple_of(kb * FOX_TK, FOX_TK), FOX_TK)
        m_out = []
        for h, hs in enumerate(heads):
            u = u_ref[slot, h]
            if diagonal:
                u = jnp.where(_causal_mask(FOX_TQ), u, -jnp.inf)
            m_new = jnp.maximum(m_old[h], jnp.max(u, axis=-1, keepdims=True))
            p = jnp.exp2((u - m_new) * FOX_EXP2_SCALE)
            alpha = jnp.exp2((m_old[h] - m_new) * FOX_EXP2_SCALE)
            v_ext = jnp.concatenate([v_ref[ks, hs], ones_ext], axis=-1)
            acc_ref[h] = alpha * acc_ref[h] + jnp.dot(p.astype(BF16), v_ext,
                                                      preferred_element_type=F32)
            m_out.append(m_new)
        return tuple(m_out)

    def body(kb, m_old):
        slot = kb % 2
        m_new = consume(kb, slot, m_old, False)
        logits_into(kb + 1, 1 - slot)
        return m_new

    logits_into(0, 0)
    m_init = tuple(jnp.full((FOX_TQ, 1), -jnp.inf, F32) for _ in heads)
    m_run = lax.fori_loop(0, qi, body, m_init)
    consume(qi, qi % 2, m_run, True)

    for h, hs in enumerate(heads):
        acc = acc_ref[h]
        o = acc[:, :HEAD_DIM] / acc[:, HEAD_DIM:]
        y_ref[:, hs] = _rms_rows(o, hn_ref[:, hs]).astype(y_ref.dtype)


def _fox(main, gates, head_norm, batch, seq):
    nq = seq // FOX_TQ
    m = main.shape[0]
    sel = _fox_bias_selectors()
    return pl.pallas_call(
        functools.partial(_fox_kernel, seq=seq),
        grid=(batch, nq),
        in_specs=[
            pl.BlockSpec((FOX_TQ, D_FOX), lambda b, i: (b * nq + i, BLK_QA)),
            pl.BlockSpec((seq, D_FOX), lambda b, i: (b, BLK_KA)),
            pl.BlockSpec((seq, D_FOX), lambda b, i: (b, BLK_VA)),
            pl.BlockSpec((seq, LANES), lambda b, i: (b, 0)),
            pl.BlockSpec(sel.shape, lambda b, i: (0, 0)),
            pl.BlockSpec((1, D_FOX), lambda b, i: (0, 0)),
        ],
        out_specs=pl.BlockSpec((FOX_TQ, D_FOX), lambda b, i: (b * nq + i, 0)),
        out_shape=jax.ShapeDtypeStruct((m, D_FOX), BF16),
        scratch_shapes=[
            pltpu.VMEM((seq, D_FOX), BF16),
            pltpu.VMEM((seq, D_FOX), BF16),
            pltpu.VMEM((2, FOX_HEADS, FOX_TQ, FOX_TK), F32),
            pltpu.VMEM((FOX_HEADS, FOX_TQ, 2 * HEAD_DIM), F32),
        ],
        compiler_params=pltpu.CompilerParams(
            dimension_semantics=("parallel", "arbitrary"),
            vmem_limit_bytes=VMEM_LIMIT),
        name="fox",
    )(main, main, main, gates, sel, head_norm)


CC_TS = 512
CC_HIST = 32
CC_ROWS = 32
CC_SHIFT_ROWS = CC_HIST + CC_TS - SUBLANES


def _cconv_kernel(u_ref, g_ref, w_ref, b_ref, lg_ref, lb_ref, y_ref, ybuf, yshift):
    si = pl.program_id(1)

    @pl.when(si == 0)
    def _():
        ybuf[0:CC_HIST, :] = jnp.zeros((CC_HIST, D_CONV), F32)

    @pl.when(si > 0)
    def _():
        ybuf[0:CC_HIST, :] = ybuf[CC_TS:CC_TS + CC_HIST, :]

    ybuf[CC_HIST:CC_HIST + CC_TS, :] = (
        u_ref[...].astype(F32) * _sigmoid(g_ref[...].astype(F32)))

    first = CC_HIST - (CONV_WIDTH - 1)
    for phase in range(1, SUBLANES):
        yshift[phase - 1] = ybuf[phase:phase + CC_SHIFT_ROWS, :]
    for r in range(CC_TS // CC_ROWS):
        acc = jnp.broadcast_to(b_ref[...], (CC_ROWS, D_CONV))
        for j in range(CONV_WIDTH):
            phase = (first + j) % SUBLANES
            lo = r * CC_ROWS + (first + j) - phase
            if phase == 0:
                tap = ybuf[lo:lo + CC_ROWS, :]
            else:
                tap = yshift[phase - 1, lo:lo + CC_ROWS, :]
            acc = acc + w_ref[j:j + 1, :] * tap
        mu = jnp.mean(acc, axis=-1, keepdims=True)
        xc = acc - mu
        yn = xc * lax.rsqrt(jnp.mean(xc * xc, axis=-1, keepdims=True) + EPS)
        yn = yn * lg_ref[...] + lb_ref[...]
        y_ref[r * CC_ROWS:(r + 1) * CC_ROWS, :] = _swish(yn).astype(y_ref.dtype)


def _cconv(main, w, b, ln_g, ln_b, batch, seq):
    ns = seq // CC_TS
    m = main.shape[0]
    return pl.pallas_call(
        _cconv_kernel,
        grid=(batch, ns),
        in_specs=[
            pl.BlockSpec((CC_TS, D_CONV), lambda bi, s: (bi * ns + s, BLK_UC)),
            pl.BlockSpec((CC_TS, D_CONV), lambda bi, s: (bi * ns + s, BLK_GC)),
            pl.BlockSpec((CONV_WIDTH, D_CONV), lambda bi, s: (0, 0)),
            pl.BlockSpec((1, D_CONV), lambda bi, s: (0, 0)),
            pl.BlockSpec((1, D_CONV), lambda bi, s: (0, 0)),
            pl.BlockSpec((1, D_CONV), lambda bi, s: (0, 0)),
        ],
        out_specs=pl.BlockSpec((CC_TS, D_CONV), lambda bi, s: (bi * ns + s, 0)),
        out_shape=jax.ShapeDtypeStruct((m, D_CONV), BF16),
        scratch_shapes=[
            pltpu.VMEM((CC_HIST + CC_TS, D_CONV), F32),
            pltpu.VMEM((SUBLANES - 1, CC_SHIFT_ROWS, D_CONV), F32),
        ],
        compiler_params=pltpu.CompilerParams(
            dimension_semantics=("parallel", "arbitrary"),
            vmem_limit_bytes=VMEM_LIMIT),
        name="cconv",
    )(main, main, w, b, ln_g, ln_b)


OUT_TM = 512
OUT_TN = 512


def _out_proj_kernel(x_ref, ym_ref, ya_ref, yc_ref, w_ref, o_ref):
    for n in range(D_MODEL // OUT_TN):
        cols = slice(n * OUT_TN, (n + 1) * OUT_TN)
        acc = jnp.dot(ym_ref[...], w_ref[0:D_MLSTM, cols], preferred_element_type=F32)
        acc = acc + jnp.dot(ya_ref[...], w_ref[D_MLSTM:D_MLSTM + D_FOX, cols],
                            preferred_element_type=F32)
        acc = acc + jnp.dot(yc_ref[...], w_ref[D_MLSTM + D_FOX:D_MODEL, cols],
                            preferred_element_type=F32)
        o_ref[:, cols] = x_ref[:, cols] + acc


def _out_proj(x2, ym, ya, yc, w_out, layer):
    m = x2.shape[0]
    return pl.pallas_call(
        _out_proj_kernel,
        grid=(m // OUT_TM,),
        in_specs=[
            pl.BlockSpec((OUT_TM, D_MODEL), lambda i: (i, 0)),
            pl.BlockSpec((OUT_TM, D_MLSTM), lambda i: (i, 0)),
            pl.BlockSpec((OUT_TM, D_FOX), lambda i: (i, 0)),
            pl.BlockSpec((OUT_TM, D_CONV), lambda i: (i, 0)),
            pl.BlockSpec((None, D_MODEL, D_MODEL), lambda i: (layer, 0, 0),
                         pipeline_mode=pl.Buffered(1)),
        ],
        out_specs=pl.BlockSpec((OUT_TM, D_MODEL), lambda i: (i, 0)),
        out_shape=jax.ShapeDtypeStruct((m, D_MODEL), F32),
        compiler_params=pltpu.CompilerParams(
            dimension_semantics=("parallel",),
            vmem_limit_bytes=VMEM_LIMIT),
        name="out_proj",
    )(x2, ym, ya, yc, w_out)


FFN_TM = 1024
FFN_TF = 1024
FFN_HALF = 512
FFN_SLAB = 128
FFN_VMEM_LIMIT = 60 * 1024 * 1024


def _ffn_kernel(x_ref, g_ref, wu_ref, wd_ref, fg_ref, o_ref, xn_ref, *, final_norm):
    j = pl.program_id(1)

    @pl.when(j == 0)
    def _():
        for r in range(FFN_TM // FFN_SLAB):
            rows = slice(r * FFN_SLAB, (r + 1) * FFN_SLAB)
            x = x_ref[rows, :]
            xn_ref[rows, :] = _rms_rows(x, g_ref[...]).astype(BF16)
            o_ref[rows, :] = x

    for part in range(FFN_TF // FFN_HALF):
        cols = slice(part * FFN_HALF, (part + 1) * FFN_HALF)
        hid = jnp.maximum(jnp.dot(xn_ref[...], wu_ref[:, cols], preferred_element_type=F32), 0.0)
        hid = (hid * hid).astype(BF16)
        o_ref[...] += jnp.dot(hid, wd_ref[cols, :], preferred_element_type=F32)

    if final_norm:
        @pl.when(j == pl.num_programs(1) - 1)
        def _():
            for r in range(FFN_TM // FFN_SLAB):
                rows = slice(r * FFN_SLAB, (r + 1) * FFN_SLAB)
                o_ref[rows, :] = _rms_rows(o_ref[rows, :], fg_ref[...])


def _ffn(x2, g, w_up, w_down, final_g, layer, final_norm):
    m = x2.shape[0]
    return pl.pallas_call(
        functools.partial(_ffn_kernel, final_norm=final_norm),
        grid=(m // FFN_TM, D_FF // FFN_TF),
        in_specs=[
            pl.BlockSpec((FFN_TM, D_MODEL), lambda i, j: (i, 0)),
            pl.BlockSpec((1, D_MODEL), lambda i, j: (0, 0)),
            pl.BlockSpec((None, D_MODEL, FFN_TF), lambda i, j: (layer, 0, j)),
            pl.BlockSpec((None, FFN_TF, D_MODEL), lambda i, j: (layer, j, 0)),
            pl.BlockSpec((1, D_MODEL), lambda i, j: (0, 0)),
        ],
        out_specs=pl.BlockSpec((FFN_TM, D_MODEL), lambda i, j: (i, 0)),
        out_shape=jax.ShapeDtypeStruct((m, D_MODEL), F32),
        scratch_shapes=[pltpu.VMEM((FFN_TM, D_MODEL), BF16)],
        compiler_params=pltpu.CompilerParams(
            dimension_semantics=("parallel", "arbitrary"),
            vmem_limit_bytes=FFN_VMEM_LIMIT),
        name="ffn",
    )(x2, g, w_up, w_down, final_g)


OFF_GATE_M = 4 * D_MLSTM
OFF_FOX = OFF_GATE_M + 2 * MLSTM_HEADS
OFF_GATE_A = OFF_FOX + 3 * D_FOX
OFF_CONV = OFF_GATE_A + FOX_HEADS
D_IN = OFF_CONV + 2 * D_CONV
MAIN_FOX = 4 * D_MLSTM
MAIN_CONV = MAIN_FOX + 3 * D_FOX
assert OFF_GATE_M % LANES == 0 and (OFF_GATE_A - GATE_FA) % LANES == 0
PREP_ROWS = 256


def _prep_in_kernel(w_ref, main_ref, gate_ref):
    main_ref[:, 0:MAIN_FOX] = w_ref[:, 0:OFF_GATE_M].astype(BF16)
    main_ref[:, MAIN_FOX:MAIN_CONV] = w_ref[:, OFF_FOX:OFF_GATE_A].astype(BF16)
    main_ref[:, MAIN_CONV:D_MAIN] = w_ref[:, OFF_CONV:D_IN].astype(BF16)
    lane = lax.broadcasted_iota(jnp.int32, (1, LANES), 1)
    gate_m = w_ref[:, OFF_GATE_M:OFF_GATE_M + LANES]
    gate_a = w_ref[:, OFF_GATE_A - GATE_FA:OFF_GATE_A - GATE_FA + LANES]
    gate = jnp.where(lane < GATE_FA, gate_m, jnp.where(lane < GATE_FA + FOX_HEADS, gate_a, 0.0))
    gate_ref[...] = gate.astype(BF16)


def _prep_in_proj(w_in):
    depth, d, _ = w_in.shape
    return pl.pallas_call(
        _prep_in_kernel,
        grid=(depth, d // PREP_ROWS),
        in_specs=[pl.BlockSpec((None, PREP_ROWS, D_IN), lambda l, i: (l, i, 0))],
        out_specs=[
            pl.BlockSpec((None, PREP_ROWS, D_MAIN), lambda l, i: (l, i, 0)),
            pl.BlockSpec((None, PREP_ROWS, LANES), lambda l, i: (l, i, 0)),
        ],
        out_shape=[
            jax.ShapeDtypeStruct((depth, d, D_MAIN), BF16),
            jax.ShapeDtypeStruct((depth, d, LANES), BF16),
        ],
        compiler_params=pltpu.CompilerParams(
            dimension_semantics=("parallel", "parallel"),
            vmem_limit_bytes=VMEM_LIMIT),
        name="prep_in_proj",
    )(w_in)


def _split_bias(b_in):
    main = jnp.concatenate([b_in[..., 0:OFF_GATE_M], b_in[..., OFF_FOX:OFF_GATE_A],
                            b_in[..., OFF_CONV:D_IN]], axis=-1)
    pad = LANES - GATE_FA - FOX_HEADS
    gate = jnp.concatenate([b_in[..., OFF_GATE_M:OFF_FOX], b_in[..., OFF_GATE_A:OFF_CONV],
                            jnp.zeros(b_in.shape[:-1] + (pad,), b_in.dtype)], axis=-1)
    return main, gate


def kernel(x, norm_mix, w_in, b_in, mlstm_conv_w, mlstm_conv_b, mlstm_head_norm, fox_head_norm,
           conv_dw_w, conv_dw_b, conv_ln_g, conv_ln_b, w_out, norm_ffn, w_up, w_down, final_norm):
    batch, seq, d = x.shape
    assert d == D_MODEL and seq % max(FOX_TQ, IN_TM, CC_TS, MLSTM_STEP_CHUNKS * MLSTM_CHUNK) == 0
    depth = w_in.shape[0]
    x2 = x.reshape(batch * seq, d)
    assert w_in.shape[1:] == (D_MODEL, D_IN)
    w_main, w_gate = _prep_in_proj(w_in)
    b_main, b_gate = _split_bias(b_in)
    w_out_b = w_out.astype(BF16)
    w_up_b = w_up.astype(BF16)
    w_down_b = w_down.astype(BF16)
    row = lambda a: a.reshape(1, -1)

    for l in range(depth):
        main, gates = _in_proj(x2, row(norm_mix[l]), w_main, row(b_main[l]), w_gate, row(b_gate[l]), l)
        ym = _mlstm(main, gates, mlstm_conv_w[l], row(mlstm_conv_b[l]),
                    row(mlstm_head_norm[l]), batch, seq)
        ya = _fox(main, gates, row(fox_head_norm[l]), batch, seq)
        yc = _cconv(main, conv_dw_w[l], row(conv_dw_b[l]), row(conv_ln_g[l]),
                    row(conv_ln_b[l]), batch, seq)
        x2 = _out_proj(x2, ym, ya, yc, w_out_b, l)
        x2 = _ffn(x2, row(norm_ffn[l]), w_up_b, w_down_b, row(final_norm), l,
                  final_norm=(l == depth - 1))
    return x2.reshape(batch, seq, d)
```
